```python
import jax, jax.numpy as jnp
from jax import lax
import numpy as np

D_MODEL = 2048
BATCH = 8
SEQ = 2048
DEPTH = 4
DEC_BATCH = 32
DEC_SEQ = 64
PAST_LEN = 4096

CHUNK = 64
MIX_WIDTH = D_MODEL
HEAD_DIM = 128
ATT_HEADS = MIX_WIDTH // (2 * HEAD_DIM)
ATT_WIDTH = ATT_HEADS * HEAD_DIM
CONV_WIDTH = MIX_WIDTH - ATT_WIDTH
CONV_GROUPS = 8
CONV_GROUP_DIM = CONV_WIDTH // CONV_GROUPS
CONV_K = 3
PROJ_WIDTH = 3 * ATT_WIDTH + 3 * CONV_WIDTH
SB_SCALE = HEAD_DIM ** -0.5
SB_QBLOCK = 128
PEER_HEADS = 8
PEER_DKEY = 256
PEER_DHALF = PEER_DKEY // 2
N_KEYS = 128
N_EXPERTS = N_KEYS * N_KEYS
PEER_TOPK = 16
PEER_TBLOCK = 128
EPS = 1e-6

kernel_name = 'stick_break_conv_peer_stream'


def rms_norm(x, g):
    xf = x.astype(jnp.float32)
    y = xf * lax.rsqrt(jnp.mean(xf * xf, axis=-1, keepdims=True) + EPS)
    return (y * g.astype(jnp.float32)).astype(x.dtype)


def sb_attention(q, k, v, q_pos, k_pos):
    b, t, h, d = q.shape
    blk = min(SB_QBLOCK, t)
    nblk = -(-t // blk)
    pad = nblk * blk - t
    qp = jnp.pad(q, ((0, 0), (0, pad), (0, 0), (0, 0)))
    pp = jnp.pad(q_pos, (0, pad))
    q_blocks = qp.reshape(b, nblk, blk, h, d).transpose(1, 0, 2, 3, 4)
    p_blocks = pp.reshape(nblk, blk)

    def one_block(args):
        qb, pb = args
        z = jnp.einsum('bqhd,bshd->bhqs', qb, k, preferred_element_type=jnp.float32) * SB_SCALE
        mask = k_pos[None, :] < pb[:, None]
        log_rest = jnp.where(mask, jax.nn.log_sigmoid(-z), 0.0)
        after = lax.cumsum(log_rest, axis=3, reverse=True) - log_rest
        w = jnp.where(mask, jnp.exp(jax.nn.log_sigmoid(z) + after), 0.0)
        return jnp.einsum('bhqs,bshd->bqhd', w.astype(v.dtype), v)

    out = lax.map(one_block, (q_blocks, p_blocks))
    return out.transpose(1, 0, 2, 3, 4).reshape(b, nblk * blk, h, d)[:, :t]


def causal_dwconv(u, prev, w):
    t = u.shape[1]
    full = jnp.concatenate([prev, u], axis=1)
    y = w[0] * full[:, 0:t]
    for i in range(1, CONV_K):
        y = y + w[i] * full[:, i:i + t]
    return y, full[:, t:]


def peer_ffn(h, w_q, keys, u_tab, v_tab):
    b, t, d = h.shape
    n = b * t
    nb = -(-n // PEER_TBLOCK)
    flat = jnp.pad(h.reshape(n, d), ((0, nb * PEER_TBLOCK - n), (0, 0))).reshape(nb, PEER_TBLOCK, d)

    def one_block(xb):
        q = (xb @ w_q).reshape(PEER_TBLOCK, PEER_HEADS, 2, PEER_DHALF)
        s = jnp.einsum('thpd,hpkd->thpk', q, keys, preferred_element_type=jnp.float32)
        s_top, i_top = lax.top_k(s, PEER_TOPK)
        cand = (s_top[:, :, 0, :, None] + s_top[:, :, 1, None, :]).reshape(
            PEER_TBLOCK, PEER_HEADS, PEER_TOPK * PEER_TOPK)
        cand_idx = (i_top[:, :, 0, :, None] * N_KEYS + i_top[:, :, 1, None, :]).reshape(
            PEER_TBLOCK, PEER_HEADS, PEER_TOPK * PEER_TOPK)
        best, pos = lax.top_k(cand, PEER_TOPK)
        idx = jnp.take_along_axis(cand_idx, pos, axis=-1)
        g = jax.nn.softmax(best, axis=-1)
        act = jax.nn.gelu(jnp.einsum('thkd,td->thk', u_tab[idx], xb))
        coef = (g * act.astype(jnp.float32)).astype(xb.dtype)
        return jnp.einsum('thk,thkd->td', coef, v_tab[idx])

    out = lax.map(one_block, flat)
    return out.reshape(nb * PEER_TBLOCK, d)[:n].reshape(b, t, d)


def trunk_layer(x, c, k_past, v_past, conv_prev, q_pos, k_pos,
                w_ada, b_ada, norm_mix, w_in, w_conv, norm_att_out, norm_conv_out, w_out,
                norm_ffn, w_peer_q, peer_keys, peer_u, peer_v):
    b, t, _ = x.shape
    mod = jax.nn.silu(c) @ w_ada + b_ada
    sh1, sc1, g1, sh2, sc2, g2 = jnp.split(mod[:, None, :], 6, axis=-1)
    h = rms_norm(x, norm_mix) * (1 + sc1) + sh1
    proj = h @ w_in
    q, k, v, bgate, cgate, xc = jnp.split(
        proj, [ATT_WIDTH, 2 * ATT_WIDTH, 3 * ATT_WIDTH,
               3 * ATT_WIDTH + CONV_WIDTH, 3 * ATT_WIDTH + 2 * CONV_WIDTH], axis=-1)
    q = q.reshape(b, t, ATT_HEADS, HEAD_DIM)
    k = k.reshape(b, t, ATT_HEADS, HEAD_DIM)
    v = v.reshape(b, t, ATT_HEADS, HEAD_DIM)
    if k_past is None:
        k_all, v_all = k, v
    else:
        k_all = jnp.concatenate([k_past, k], axis=1)
        v_all = jnp.concatenate([v_past, v], axis=1)
    att = sb_attention(q, k_all, v_all, q_pos, k_pos)
    att = rms_norm(att, norm_att_out.reshape(ATT_HEADS, HEAD_DIM)).reshape(b, t, ATT_WIDTH)
    conv_out, conv_new = causal_dwconv(cgate * xc, conv_prev, w_conv)
    cv = rms_norm((bgate * conv_out).reshape(b, t, CONV_GROUPS, CONV_GROUP_DIM),
                  norm_conv_out.reshape(CONV_GROUPS, CONV_GROUP_DIM)).reshape(b, t, CONV_WIDTH)
    x = x + g1 * (jnp.concatenate([att, cv], axis=-1) @ w_out)
    h2 = rms_norm(x, norm_ffn) * (1 + sc2) + sh2
    x = x + g2 * peer_ffn(h2, w_peer_q, peer_keys, peer_u, peer_v)
    return x, k, v, conv_new


def setup_inputs(seed: int = 0) -> dict:
    key = jax.random.key(seed)
    ks = jax.random.split(key, 22)
    f32 = jnp.float32
    nrm = lambda k, shape, s: jax.random.normal(k, shape, f32) * s
    return {
        'x_prompt': nrm(ks[0], (BATCH, SEQ, D_MODEL), 1.0),
        'x_sample': nrm(ks[1], (DEC_BATCH, DEC_SEQ, D_MODEL), 1.0),
        'cache_k': nrm(ks[2], (DEPTH, DEC_BATCH, PAST_LEN, ATT_HEADS, HEAD_DIM), 1.0),
        'cache_v': nrm(ks[3], (DEPTH, DEC_BATCH, PAST_LEN, ATT_HEADS, HEAD_DIM), 1.0),
        'state_conv': nrm(ks[4], (DEPTH, DEC_BATCH, CONV_K - 1, CONV_WIDTH), 1.0),
        'c_prompt': nrm(ks[5], (BATCH, D_MODEL), 1.0),
        'c_sample': nrm(ks[6], (DEC_BATCH, D_MODEL), 1.0),
        'w_ada': nrm(ks[7], (DEPTH, D_MODEL, 6 * D_MODEL), 0.5 * D_MODEL ** -0.5),
        'b_ada': nrm(ks[8], (DEPTH, 6 * D_MODEL), 0.02),
        'norm_mix': 1.0 + nrm(ks[9], (DEPTH, D_MODEL), 0.05),
        'w_in': nrm(ks[10], (DEPTH, D_MODEL, PROJ_WIDTH), D_MODEL ** -0.5),
        'w_conv': nrm(ks[11], (DEPTH, CONV_K, CONV_WIDTH), CONV_K ** -0.5),
        'norm_att_out': 1.0 + nrm(ks[12], (DEPTH, ATT_WIDTH), 0.05),
        'norm_conv_out': 1.0 + nrm(ks[13], (DEPTH, CONV_WIDTH), 0.05),
        'w_out': nrm(ks[14], (DEPTH, MIX_WIDTH, D_MODEL), MIX_WIDTH ** -0.5),
        'norm_ffn': 1.0 + nrm(ks[15], (DEPTH, D_MODEL), 0.05),
        'w_peer_q': nrm(ks[16], (DEPTH, D_MODEL, PEER_HEADS * PEER_DKEY), D_MODEL ** -0.5),
        'peer_keys': nrm(ks[17], (DEPTH, PEER_HEADS, 2, N_KEYS, PEER_DHALF), PEER_DHALF ** -0.5),
        'peer_u': nrm(ks[18], (DEPTH, N_EXPERTS, D_MODEL), D_MODEL ** -0.5),
        'peer_v': nrm(ks[19], (DEPTH, N_EXPERTS, D_MODEL), PEER_HEADS ** -0.5),
        'norm_final': 1.0 + nrm(ks[20], (D_MODEL,), 0.05),
    }


def reference(x_prompt, x_sample, cache_k, cache_v, state_conv, c_prompt, c_sample,
              w_ada, b_ada, norm_mix, w_in, w_conv, norm_att_out, norm_conv_out, w_out,
              norm_ffn, w_peer_q, peer_keys, peer_u, peer_v, norm_final):
    t_p = x_prompt.shape[1]
    t_s = x_sample.shape[1]
    past = cache_k.shape[2]
    pos_p = jnp.arange(t_p, dtype=jnp.int32)
    pos_sq = past + jnp.arange(t_s, dtype=jnp.int32)
    pos_sk = jnp.arange(past + t_s, dtype=jnp.int32)
    conv_zero = jnp.zeros((x_prompt.shape[0], CONV_K - 1, CONV_WIDTH), x_prompt.dtype)
    xp, xs = x_prompt, x_sample
    kp_l, vp_l, cp_l, ks_l, vs_l, cs_l = [], [], [], [], [], []
    for l in range(DEPTH):
        wl = (w_ada[l], b_ada[l], norm_mix[l], w_in[l], w_conv[l], norm_att_out[l],
              norm_conv_out[l], w_out[l], norm_ffn[l], w_peer_q[l], peer_keys[l],
              peer_u[l], peer_v[l])
        xp, kp, vp, cp = trunk_layer(xp, c_prompt, None, None, conv_zero, pos_p, pos_p, *wl)
        xs, ksm, vsm, csm = trunk_layer(xs, c_sample, cache_k[l], cache_v[l], state_conv[l],
                                        pos_sq, pos_sk, *wl)
        kp_l.append(kp); vp_l.append(vp); cp_l.append(cp)
        ks_l.append(ksm); vs_l.append(vsm); cs_l.append(csm)
    y_prompt = rms_norm(xp, norm_final)
    y_sample = rms_norm(xs, norm_final)
    return (y_prompt, y_sample, jnp.stack(kp_l), jnp.stack(vp_l), jnp.stack(cp_l),
            jnp.stack(ks_l), jnp.stack(vs_l), jnp.stack(cs_l))
```

```python
import functools
import math

import jax
import jax.numpy as jnp
from jax import lax
from jax.experimental import pallas as pl
from jax.experimental.pallas import tpu as pltpu

F32 = jnp.float32
BF16 = jnp.bfloat16
EPS = 1e-6
LANES = 128
SUBLANES = 8
CONV_K = 3
PEER_TOPK = 16
VMEM_LIMIT_BYTES = 56 * 1024 * 1024
EXP_UNDERFLOW = -104.0


def _params(*sem):
    return pltpu.CompilerParams(dimension_semantics=sem, vmem_limit_bytes=VMEM_LIMIT_BYTES)


def _pick(n, cands):
    for c in cands:
        if n % c == 0:
            return c
    raise ValueError(f"no tile in {cands} divides {n}")


def _ada_kernel(c_ref, w_ref, b_ref, o_ref):
    c = c_ref[...]
    s = (c * jax.nn.sigmoid(c)).astype(BF16)
    o_ref[...] = jnp.dot(s, w_ref[...].astype(BF16), preferred_element_type=F32) + b_ref[...]


def _ada(c_all, w_ada, b_ada):
    depth, d, n6 = w_ada.shape
    r = c_all.shape[0]
    tn = _pick(n6, (1024, 512, 256, 128))
    return pl.pallas_call(
        _ada_kernel,
        grid=(depth, n6 // tn),
        in_specs=[
            pl.BlockSpec((r, d), lambda l, j: (0, 0)),
            pl.BlockSpec((None, d, tn), lambda l, j: (l, 0, j)),
            pl.BlockSpec((None, 1, tn), lambda l, j: (l, 0, j)),
        ],
        out_specs=pl.BlockSpec((None, r, tn), lambda l, j: (l, 0, j)),
        out_shape=jax.ShapeDtypeStruct((depth, r, n6), F32),
        compiler_params=_params("parallel", "parallel"),
    )(c_all, w_ada, b_ada.reshape(depth, 1, n6))


def _nmm_kernel(x_ref, g_ref, sc_ref, sh_ref, w_ref, o_ref, *rest, group, emit_h):
    hs_ref = rest[-1]
    tr = x_ref.shape[0]

    @pl.when(pl.program_id(1) == 0)
    def _():
        x = x_ref[...]
        y = x * lax.rsqrt(jnp.mean(x * x, axis=-1, keepdims=True) + EPS) * g_ref[...]
        for g in range(tr // group):
            rows = slice(g * group, (g + 1) * group)
            hs_ref[rows, :] = (y[rows] * (1.0 + sc_ref[g]) + sh_ref[g]).astype(BF16)
        if emit_h:
            rest[0][...] = hs_ref[...]

    o_ref[...] = jnp.dot(hs_ref[...], w_ref[...], preferred_element_type=F32)


def _norm_mod_matmul(x, gain, sc, sh, w_bf16, *, group, emit_h):
    n, d = x.shape
    nout = w_bf16.shape[1]
    tr = _pick(n, (512, 256, 128, 64))
    tn = _pick(nout, (1024, 512, 256, 128))
    gpt = tr // group
    out_shape = [jax.ShapeDtypeStruct((n, nout), F32)]
    out_specs = [pl.BlockSpec((tr, tn), lambda i, j: (i, j))]
    if emit_h:
        out_shape.append(jax.ShapeDtypeStruct((n, d), BF16))
        out_specs.append(pl.BlockSpec((tr, d), lambda i, j: (i, 0)))
    res = pl.pallas_call(
        functools.partial(_nmm_kernel, group=group, emit_h=emit_h),
        grid=(n // tr, nout // tn),
        in_specs=[
            pl.BlockSpec((tr, d), lambda i, j: (i, 0)),
            pl.BlockSpec((1, d), lambda i, j: (0, 0)),
            pl.BlockSpec((gpt, 1, d), lambda i, j: (i, 0, 0)),
            pl.BlockSpec((gpt, 1, d), lambda i, j: (i, 0, 0)),
            pl.BlockSpec((d, tn), lambda i, j: (0, j)),
        ],
        out_specs=out_specs,
        out_shape=out_shape,
        scratch_shapes=[pltpu.VMEM((tr, d), BF16)],
        compiler_params=_params("parallel", "arbitrary"),
    )(x, gain.reshape(1, d), sc, sh, w_bf16)
    return res if emit_h else res[0]


def _strict_upper(n):
    return (lax.broadcasted_iota(jnp.int32, (n, n), 0) > lax.broadcasted_iota(jnp.int32, (n, n), 1)).astype(BF16)


def _sb_block(q, k, v, tri, carry, acc, *, masked, scale):
    z = lax.dot_general(q, k, (((1,), (1,)), ((), ())), preferred_element_type=F32) * scale
    ls = jnp.minimum(z, 0.0) - jnp.log1p(jnp.exp(-jnp.abs(z)))
    lr = ls - z
    if masked:
        vis = (lax.broadcasted_iota(jnp.int32, z.shape, 1) < lax.broadcasted_iota(jnp.int32, z.shape, 0))
        lr = jnp.where(vis, lr, 0.0)
    hi = lr.astype(BF16)
    r1 = lr - hi.astype(F32)
    mid = r1.astype(BF16)
    lo = (r1 - mid.astype(F32)).astype(BF16)
    suf = (jnp.dot(hi, tri, preferred_element_type=F32) + jnp.dot(mid, tri, preferred_element_type=F32)
           + jnp.dot(lo, tri, preferred_element_type=F32))
    w = jnp.exp(ls + suf + carry)
    if masked:
        w = jnp.where(vis, w, 0.0)
    acc = acc + jnp.dot(w.astype(BF16), v, preferred_element_type=F32)
    carry = carry + jnp.sum(lr, axis=1, keepdims=True)
    return carry, acc


def _attn_prompt_kernel(q_ref, k_ref, v_ref, o_ref, *, scale):
    tq, dh = q_ref.shape
    qi = pl.program_id(2)
    q = q_ref[...].astype(BF16)
    tri = _strict_upper(tq)

    def kv(kb):
        start = pl.multiple_of(kb * tq, tq)
        return k_ref[pl.ds(start, tq), :].astype(BF16), v_ref[pl.ds(start, tq), :].astype(BF16)

    k, v = kv(qi)
    carry, acc = _sb_block(q, k, v, tri, jnp.zeros((tq, 1), F32), jnp.zeros((tq, dh), F32),
                           masked=True, scale=scale)

    def cond(st):
        kb, carry, _ = st
        return jnp.logical_and(kb >= 0, jnp.max(carry) > EXP_UNDERFLOW)

    def body(st):
        kb, carry, acc = st
        k, v = kv(kb)
        carry, acc = _sb_block(q, k, v, tri, carry, acc, masked=False, scale=scale)
        return kb - 1, carry, acc

    _, _, acc = lax.while_loop(cond, body, (qi - 1, carry, acc))
    o_ref[...] = acc


def _attn_prompt(proj, *, batch, seq, heads, scale):
    tq = _pick(seq, (256, 128, 64))
    nq = seq // tq
    return pl.pallas_call(
        functools.partial(_attn_prompt_kernel, scale=scale),
        grid=(batch, heads, nq),
        in_specs=[
            pl.BlockSpec((tq, LANES), lambda b, h, i: (b * nq + i, h)),
            pl.BlockSpec((seq, LANES), lambda b, h, i: (b, heads + h)),
            pl.BlockSpec((seq, LANES), lambda b, h, i: (b, 2 * heads + h)),
        ],
        out_specs=pl.BlockSpec((tq, LANES), lambda b, h, i: (b * nq + i, h)),
        out_shape=jax.ShapeDtypeStruct((batch * seq, heads * LANES), F32),
        compiler_params=_params("parallel", "parallel", "parallel"),
    )(proj, proj, proj)


def _attn_sample_kernel(q_ref, kn_ref, vn_ref, kc_ref, vc_ref, o_ref, carry_ref, acc_ref, *, heads, tk, scale):
    tq = q_ref.shape[0]
    tkp = kc_ref.shape[0]
    j = pl.program_id(1)

    def hs(h):
        return slice(h * LANES, (h + 1) * LANES)

    @pl.when(j == 0)
    def _():
        tri = _strict_upper(tq)
        for h in range(heads):
            carry, acc = _sb_block(q_ref[:, hs(h)].astype(BF16), kn_ref[:, hs(h)].astype(BF16),
                                   vn_ref[:, hs(h)].astype(BF16), tri,
                                   jnp.zeros((tq, 1), F32), jnp.zeros((tq, LANES), F32),
                                   masked=True, scale=scale)
            carry_ref[h] = jnp.broadcast_to(carry, (tq, LANES))
            acc_ref[:, hs(h)] = acc

    @pl.when(j > 0)
    def _():
        tri = _strict_upper(tk)
        for h in range(heads):
            for sb in reversed(range(tkp // tk)):
                carry = carry_ref[h][:, 0:1]

                @pl.when(jnp.max(carry) > EXP_UNDERFLOW)
                def _():
                    rows = slice(sb * tk, (sb + 1) * tk)
                    c2, a2 = _sb_block(q_ref[:, hs(h)].astype(BF16), kc_ref[rows, hs(h)].astype(BF16),
                                       vc_ref[rows, hs(h)].astype(BF16), tri, carry, acc_ref[:, hs(h)],
                                       masked=False, scale=scale)
                    carry_ref[h] = jnp.broadcast_to(c2, (tq, LANES))
                    acc_ref[:, hs(h)] = a2

    @pl.when(j == pl.num_programs(1) - 1)
    def _():
        o_ref[...] = acc_ref[...]


def _attn_sample(proj, cache_k, cache_v, layer, *, n_prompt, batch, seq, heads, scale):
    w = heads * LANES
    past = cache_k.shape[2]
    tkp = _pick(past, (512, 256, 128))
    tk = min(tkp, 256)
    n_past = past // tkp
    row0 = n_prompt // seq
    cache_spec = pl.BlockSpec((None, None, tkp, w), lambda b, j: (layer, b, n_past - jnp.maximum(j, 1), 0))
    return pl.pallas_call(
        functools.partial(_attn_sample_kernel, heads=heads, tk=tk, scale=scale),
        grid=(batch, 1 + n_past),
        in_specs=[
            pl.BlockSpec((seq, w), lambda b, j: (row0 + b, 0)),
            pl.BlockSpec((seq, w), lambda b, j: (row0 + b, 1)),
            pl.BlockSpec((seq, w), lambda b, j: (row0 + b, 2)),
            cache_spec,
            cache_spec,
        ],
        out_specs=pl.BlockSpec((seq, w), lambda b, j: (b, 0)),
        out_shape=jax.ShapeDtypeStruct((batch * seq, w), F32),
        scratch_shapes=[pltpu.VMEM((heads, seq, LANES), F32), pltpu.VMEM((seq, w), F32)],
        compiler_params=_params("parallel", "arbitrary"),
    )(proj, proj, proj, cache_k, cache_v)


def _mix_kernel(att_ref, bg_ref, cg_ref, xc_ref, cgh_ref, xch_ref, st_ref, wc_ref, na_ref, nc_ref,
                wo_ref, x_ref, g1_ref, o_ref, tail_ref, um1_ref, um2_ref, mix_ref,
                *, group, prompt_groups, groups_per_seq):
    tr, w = att_ref.shape
    gpt = tr // group
    i = pl.program_id(0)

    u = cg_ref[...] * xc_ref[...]
    uh = cgh_ref[...] * xch_ref[...]
    rows = lax.broadcasted_iota(jnp.int32, (tr, w), 0)
    h7 = uh[SUBLANES - 1:SUBLANES, :]
    h6 = uh[SUBLANES - 2:SUBLANES - 1, :]
    um1_ref[...] = jnp.where(rows == 0, h7, pltpu.roll(u, 1, axis=0))
    um2_ref[...] = jnp.where(rows == 0, h6, jnp.where(rows == 1, h7, pltpu.roll(u, 2, axis=0)))
    r8 = lax.broadcasted_iota(jnp.int32, (SUBLANES, w), 0)
    for g in range(gpt):
        gg = i * gpt + g
        starts_sequence = jnp.logical_or(gg >= prompt_groups, gg % groups_per_seq == 0)
        head = slice(g * group, g * group + SUBLANES)

        @pl.when(starts_sequence)
        def _():
            st0 = st_ref[g, 0:1, :]
            st1 = st_ref[g, 1:2, :]
            um1_ref[head, :] = jnp.where(r8 == 0, st1, um1_ref[head, :])
            um2_ref[head, :] = jnp.where(r8 == 0, st0, jnp.where(r8 == 1, st1, um2_ref[head, :]))

        tail_ref[g] = u[(g + 1) * group - SUBLANES:(g + 1) * group, :]

    conv = wc_ref[0:1, :] * um2_ref[...] + wc_ref[1:2, :] * um1_ref[...] + wc_ref[2:3, :] * u
    cv = bg_ref[...] * conv
    att = att_ref[...]
    for k in range(w // LANES):
        cs = slice(k * LANES, (k + 1) * LANES)
        a = att[:, cs]
        mix_ref[:, cs] = (a * lax.rsqrt(jnp.mean(a * a, axis=-1, keepdims=True) + EPS)
                          * na_ref[:, cs]).astype(BF16)
        c = cv[:, cs]
        mix_ref[:, w + k * LANES:w + (k + 1) * LANES] = (
            c * lax.rsqrt(jnp.mean(c * c, axis=-1, keepdims=True) + EPS) * nc_ref[:, cs]).astype(BF16)
    y = jnp.dot(mix_ref[...], wo_ref[...], preferred_element_type=F32)
    for g in range(gpt):
        rs = slice(g * group, (g + 1) * group)
        o_ref[rs, :] = x_ref[rs, :] + g1_ref[g] * y[rs]


def _mix_out(att, proj, state, w_conv, norm_att, norm_conv, w_out_bf16, x, g1, *, group, n_prompt, seq):
    n, d = x.shape
    w = att.shape[1]
    tr = _pick(n, (256, 128, 64))
    gpt = tr // group
    hb = tr // SUBLANES
    kern = functools.partial(_mix_kernel, group=group, prompt_groups=n_prompt // group,
                             groups_per_seq=seq // group)
    halo = lambda col: pl.BlockSpec((SUBLANES, w), lambda i: (jnp.maximum(i * hb - 1, 0), col))
    return pl.pallas_call(
        kern,
        grid=(n // tr,),
        in_specs=[
            pl.BlockSpec((tr, w), lambda i: (i, 0)),
            pl.BlockSpec((tr, w), lambda i: (i, 3)),
            pl.BlockSpec((tr, w), lambda i: (i, 4)),
            pl.BlockSpec((tr, w), lambda i: (i, 5)),
            halo(4),
            halo(5),
            pl.BlockSpec((gpt, CONV_K - 1, w), lambda i: (i, 0, 0)),
            pl.BlockSpec((CONV_K, w), lambda i: (0, 0)),
            pl.BlockSpec((1, w), lambda i: (0, 0)),
            pl.BlockSpec((1, w), lambda i: (0, 0)),
            pl.BlockSpec((2 * w, d), lambda i: (0, 0)),
            pl.BlockSpec((tr, d), lambda i: (i, 0)),
            pl.BlockSpec((gpt, 1, d), lambda i: (i, 0, 0)),
        ],
        out_specs=[
            pl.BlockSpec((tr, d), lambda i: (i, 0)),
            pl.BlockSpec((gpt, SUBLANES, w), lambda i: (i, 0, 0)),
        ],
        out_shape=[
            jax.ShapeDtypeStruct((n, d), F32),
            jax.ShapeDtypeStruct((n // group, SUBLANES, w), F32),
        ],
        scratch_shapes=[pltpu.VMEM((tr, w), F32), pltpu.VMEM((tr, w), F32), pltpu.VMEM((tr, 2 * w), BF16)],
        compiler_params=_params("parallel"),
    )(att, proj, proj, proj, proj, proj, state, w_conv, norm_att.reshape(1, w), norm_conv.reshape(1, w),
      w_out_bf16, x, g1)


def _staircase():
    return [(r, c) for r in range(PEER_TOPK) for c in range(PEER_TOPK) if (r + 1) * (c + 1) <= PEER_TOPK]


def _peer_select_kernel(q_ref, key_ref, ea_ref, lr_ref, eb_ref, cr_ref, s_ref, rk_ref, a_ref, l_ref,
                        *, heads):
    nk, tt = s_ref.shape[1], s_ref.shape[2]
    hp = pl.program_id(1)
    slot = (hp % 2) * heads + hp // 2

    s = lax.dot_general(key_ref[...], q_ref[...], (((1,), (1,)), ((), ())),
                        precision=lax.Precision.HIGHEST, preferred_element_type=F32)
    s_ref[slot] = s
    kidx = lax.broadcasted_iota(jnp.int32, (nk, tt), 0).astype(F32)
    work = s
    rank = jnp.full((nk, tt), float(PEER_TOPK), F32)
    for r in range(PEER_TOPK):
        m = jnp.max(work, axis=0, keepdims=True)
        first = jnp.min(jnp.where(work == m, kidx, float(nk)), axis=0, keepdims=True)
        hit = kidx == first
        rank = jnp.where(hit, float(r), rank)
        work = jnp.where(hit, -jnp.inf, work)
        a_ref[r, pl.ds(slot, 1), :] = m
    rk_ref[slot] = rank

    @pl.when(hp == pl.num_programs(1) - 1)
    def _():
        cands = _staircase()
        a0 = [a_ref[r, 0:heads, :] for r in range(PEER_TOPK)]
        a1 = [a_ref[c, heads:2 * heads, :] for c in range(PEER_TOPK)]
        sums = [a0[r] + a1[c] for r, c in cands]
        beaten_by = [0.0] * len(cands)
        for xi, (rx, cx) in enumerate(cands):
            for yi in range(xi):
                ry, cy = cands[yi]
                if ry <= rx and cy <= cx:
                    beaten_by[xi] = beaten_by[xi] + 1.0
                else:
                    y_ahead = sums[yi] >= sums[xi]
                    beaten_by[xi] = beaten_by[xi] + jnp.where(y_ahead, 1.0, 0.0)
                    beaten_by[yi] = beaten_by[yi] + jnp.where(y_ahead, 0.0, 1.0)
        zsum = jnp.zeros_like(sums[0])
        kept_in_row = [jnp.zeros_like(sums[0]) for _ in range(PEER_TOPK)]
        for xi, (rx, cx) in enumerate(cands):
            keep = jnp.where(beaten_by[xi] < float(PEER_TOPK), 1.0, 0.0)
            kept_in_row[rx] = kept_in_row[rx] + keep
            zsum = zsum + keep * jnp.exp(sums[xi] - sums[0])
        for r in range(PEER_TOPK):
            l_ref[r] = kept_in_row[r]
        l_ref[PEER_TOPK] = 1.0 / zsum
        for h in range(heads):
            rk0 = rk_ref[h]
            lrow = jnp.zeros((nk, tt), F32)
            for r in range(PEER_TOPK):
                lrow = jnp.where(rk0 == float(r), l_ref[r, h:h + 1, :], lrow)
            lr_ref[h] = lrow
            ea_ref[h] = jnp.exp(s_ref[h] - a_ref[0, h:h + 1, :]) * l_ref[PEER_TOPK, h:h + 1, :]
            eb_ref[h] = jnp.exp(s_ref[heads + h] - a_ref[0, heads + h:heads + h + 1, :])
            cr_ref[h] = rk_ref[heads + h]


def _peer_select(q, keys):
    n = q.shape[0]
    heads, _, nk, dk = keys.shape
    tt = _pick(n, (256, 128))
    out = jax.ShapeDtypeStruct((heads, nk, n), F32)
    ospec = pl.BlockSpec((heads, nk, tt), lambda t, hp: (0, 0, t))
    return pl.pallas_call(
        functools.partial(_peer_select_kernel, heads=heads),
        grid=(n // tt, 2 * heads),
        in_specs=[
            pl.BlockSpec((tt, dk), lambda t, hp: (t, hp)),
            pl.BlockSpec((None, nk, dk), lambda t, hp: (hp, 0, 0)),
        ],
        out_specs=[ospec, ospec, ospec, ospec],
        out_shape=[out, out, out, out],
        scratch_shapes=[
            pltpu.VMEM((2 * heads, nk, tt), F32),
            pltpu.VMEM((2 * heads, nk, tt), F32),
            pltpu.VMEM((PEER_TOPK, 2 * heads, tt), F32),
            pltpu.VMEM((PEER_TOPK + 1, heads, tt), F32),
        ],
        compiler_params=_params("parallel", "arbitrary"),
    )(q, keys.reshape(2 * heads, nk, dk))


def _peer_dense_kernel(h_ref, u_ref, vt_ref, ea_ref, lr_ref, eb_ref, cr_ref, o_ref, acc_ref, act_ref, coef_ref,
                       *, heads):
    ec, tt = act_ref.shape
    c = pl.program_id(1)

    @pl.when(c == 0)
    def _():
        acc_ref[...] = jnp.zeros_like(acc_ref)

    act_ref[...] = lax.dot_general(u_ref[...], h_ref[...], (((1,), (1,)), ((), ())),
                                   preferred_element_type=F32)

    def one_i(il, carry):
        rows = pl.ds(pl.multiple_of(il * LANES, LANES), LANES)
        wsum = jnp.zeros((LANES, tt), F32)
        for h in range(heads):
            lrow = lr_ref[h, pl.ds(il, 1), :]
            ea = ea_ref[h, pl.ds(il, 1), :]
            wsum = wsum + jnp.where(cr_ref[h] < lrow, eb_ref[h] * ea, 0.0)
        coef_ref[rows, :] = (wsum * jax.nn.gelu(act_ref[rows, :], approximate=True)).astype(BF16)
        return carry

    lax.fori_loop(0, ec // LANES, one_i, 0)
    acc_ref[...] += jnp.dot(vt_ref[...], coef_ref[...], preferred_element_type=F32)

    @pl.when(c == pl.num_programs(1) - 1)
    def _():
        o_ref[...] = acc_ref[...].T


def _peer_dense(h_bf16, u_bf16, vt_bf16, ea, lr, eb, cr):
    n, d = h_bf16.shape
    e = u_bf16.shape[0]
    heads, nk, _ = ea.shape
    tt = _pick(n, (512, 256, 128))
    ec = _pick(e, (1024,))
    ipc = ec // LANES
    sel_i = pl.BlockSpec((heads, ipc, tt), lambda t, c: (0, c, t))
    sel_j = pl.BlockSpec((heads, nk, tt), lambda t, c: (0, 0, t))
    return pl.pallas_call(
        functools.partial(_peer_dense_kernel, heads=heads),
        grid=(n // tt, e // ec),
        in_specs=[
            pl.BlockSpec((tt, d), lambda t, c: (t, 0)),
            pl.BlockSpec((ec, d), lambda t, c: (c, 0)),
            pl.BlockSpec((d, ec), lambda t, c: (0, c)),
            sel_i, sel_i, sel_j, sel_j,
        ],
        out_specs=pl.BlockSpec((tt, d), lambda t, c: (t, 0)),
        out_shape=jax.ShapeDtypeStruct((n, d), F32),
        scratch_shapes=[pltpu.VMEM((d, tt), F32), pltpu.VMEM((ec, tt), F32), pltpu.VMEM((ec, tt), BF16)],
        compiler_params=_params("parallel", "arbitrary"),
    )(h_bf16, u_bf16, vt_bf16, ea, lr, eb, cr)


def _resid_kernel(x_ref, y_ref, g_ref, o_ref, *, group):
    for g in range(x_ref.shape[0] // group):
        rs = slice(g * group, (g + 1) * group)
        o_ref[rs, :] = x_ref[rs, :] + g_ref[g] * y_ref[rs, :]


def _gated_residual(x, y, gate, *, group):
    n, d = x.shape
    tr = _pick(n, (512, 256, 128, 64))
    row = pl.BlockSpec((tr, d), lambda i: (i, 0))
    return pl.pallas_call(
        functools.partial(_resid_kernel, group=group),
        grid=(n // tr,),
        in_specs=[row, row, pl.BlockSpec((tr // group, 1, d), lambda i: (i, 0, 0))],
        out_specs=row,
        out_shape=jax.ShapeDtypeStruct((n, d), F32),
        compiler_params=_params("parallel"),
    )(x, y, gate)


def _final_norm_kernel(x_ref, g_ref, o_ref):
    x = x_ref[...]
    o_ref[...] = x * lax.rsqrt(jnp.mean(x * x, axis=-1, keepdims=True) + EPS) * g_ref[...]


def _final_norm(x, gain):
    n, d = x.shape
    tr = _pick(n, (512, 256, 128, 64))
    row = pl.BlockSpec((tr, d), lambda i: (i, 0))
    return pl.pallas_call(
        _final_norm_kernel,
        grid=(n // tr,),
        in_specs=[row, pl.BlockSpec((1, d), lambda i: (0, 0))],
        out_specs=row,
        out_shape=jax.ShapeDtypeStruct((n, d), F32),
        compiler_params=_params("parallel"),
    )(x, gain.reshape(1, d))


def kernel(x_prompt, x_sample, cache_k, cache_v, state_conv, c_prompt, c_sample, w_ada, b_ada, norm_mix,
           w_in, w_conv, norm_att_out, norm_conv_out, w_out, norm_ffn, w_peer_q, peer_keys, peer_u, peer_v,
           norm_final):
    bp, tp, d = x_prompt.shape
    bs, ts, _ = x_sample.shape
    depth = w_in.shape[0]
    past = cache_k.shape[2]
    heads = cache_k.shape[3]
    w = heads * LANES
    assert cache_k.shape[4] == LANES and state_conv.shape[-1] == w and w_in.shape[2] == 6 * w
    assert tp % ts == 0 and ts % SUBLANES == 0
    group = ts
    n_p, n_s = bp * tp, bs * ts
    n = n_p + n_s
    scale = float(LANES) ** -0.5

    x = jnp.concatenate([x_prompt.reshape(n_p, d), x_sample.reshape(n_s, d)], axis=0)

    c_all = jnp.concatenate([c_prompt, c_sample], axis=0)
    r = c_all.shape[0]
    r_pad = -(-r // SUBLANES) * SUBLANES
    mod = _ada(jnp.pad(c_all, ((0, r_pad - r), (0, 0))), w_ada, b_ada)
    group_row = jnp.concatenate([jnp.repeat(jnp.arange(bp), tp // group), bp + jnp.arange(bs)])
    mod = jnp.take(mod, group_row, axis=1).reshape(depth, n // group, 6, 1, d)

    ck = cache_k.reshape(depth, bs, past, w)
    cv = cache_v.reshape(depth, bs, past, w)
    state = jnp.concatenate([jnp.zeros((depth, n_p // group, CONV_K - 1, w), F32), state_conv], axis=1)
    seq_end_group = jnp.concatenate([(jnp.arange(bp) + 1) * (tp // group) - 1, n_p // group + jnp.arange(bs)])

    kp, vp, cp, ks, vs, cs = [], [], [], [], [], []
    for l in range(depth):
        sh1, sc1, g1, sh2, sc2, g2 = (mod[l, :, k] for k in range(6))
        proj = _norm_mod_matmul(x, norm_mix[l], sc1, sh1, w_in[l].astype(BF16), group=group, emit_h=False)
        att_p = _attn_prompt(proj, batch=bp, seq=tp, heads=heads, scale=scale)
        att_s = _attn_sample(proj, ck, cv, l, n_prompt=n_p, batch=bs, seq=ts, heads=heads, scale=scale)
        att = jnp.concatenate([att_p, att_s], axis=0)
        x, tails = _mix_out(att, proj, state[l], w_conv[l], norm_att_out[l], norm_conv_out[l],
                            w_out[l].astype(BF16), x, g1, group=group, n_prompt=n_p, seq=tp)
        q, h2 = _norm_mod_matmul(x, norm_ffn[l], sc2, sh2, w_peer_q[l].astype(BF16), group=group, emit_h=True)
        ea, lr, eb, cr = _peer_select(q, peer_keys[l])
        y = _peer_dense(h2, peer_u[l].astype(BF16), peer_v[l].astype(BF16).T, ea, lr, eb, cr)
        x = _gated_residual(x, y, g2, group=group)

        k_new, v_new = proj[:, w:2 * w], proj[:, 2 * w:3 * w]
        conv_new = jnp.take(tails, seq_end_group, axis=0)[:, SUBLANES - (CONV_K - 1):, :]
        kp.append(k_new[:n_p].reshape(bp, tp, heads, LANES))
        vp.append(v_new[:n_p].reshape(bp, tp, heads, LANES))
        ks.append(k_new[n_p:].reshape(bs, ts, heads, LANES))
        vs.append(v_new[n_p:].reshape(bs, ts, heads, LANES))
        cp.append(conv_new[:bp])
        cs.append(conv_new[bp:])

    y = _final_norm(x, norm_final)
    return (y[:n_p].reshape(bp, tp, d), y[n_p:].reshape(bs, ts, d), jnp.stack(kp), jnp.stack(vp),
            jnp.stack(cp), jnp.stack(ks), jnp.stack(vs), jnp.stack(cs))
```

```python
import functools
import math

import jax
import jax.numpy as jnp
from jax import lax
from jax.experimental import pallas as pl
from jax.experimental.pallas import tpu as pltpu

F32 = jnp.float32
BF16 = jnp.bfloat16
EPS = 1e-6
LANES = 128
SUBLANES = 8
CONV_K = 3
PEER_TOPK = 16
VMEM_LIMIT_BYTES = 56 * 1024 * 1024
EXP_UNDERFLOW = -104.0


def _params(*sem):
    return pltpu.CompilerParams(dimension_semantics=sem, vmem_limit_bytes=VMEM_LIMIT_BYTES)


def _pick(n, cands):
    for c in cands:
        if n % c == 0:
            return c
    raise ValueError(f"no tile in {cands} divides {n}")


def _ada_kernel(c_ref, w_ref, b_ref, o_ref):
    c = c_ref[...]
    s = (c * jax.nn.sigmoid(c)).astype(BF16)
    o_ref[...] = jnp.dot(s, w_ref[...].astype(BF16), preferred_element_type=F32) + b_ref[...]


def _ada(c_all, w_ada, b_ada):
    depth, d, n6 = w_ada.shape
    r = c_all.shape[0]
    tn = _pick(n6, (1024, 512, 256, 128))
    return pl.pallas_call(
        _ada_kernel,
        grid=(depth, n6 // tn),
        in_specs=[
            pl.BlockSpec((r, d), lambda l, j: (0, 0)),
            pl.BlockSpec((None, d, tn), lambda l, j: (l, 0, j)),
            pl.BlockSpec((None, 1, tn), lambda l, j: (l, 0, j)),
        ],
        out_specs=pl.BlockSpec((None, r, tn), lambda l, j: (l, 0, j)),
        out_shape=jax.ShapeDtypeStruct((depth, r, n6), F32),
        name="ada",
        compiler_params=_params("parallel", "parallel"),
    )(c_all, w_ada, b_ada.reshape(depth, 1, n6))


def _nmm_kernel(x_ref, g_ref, sc_ref, sh_ref, w_ref, o_ref, *rest, group, emit_h):
    hs_ref = rest[-1]
    tr = x_ref.shape[0]

    @pl.when(pl.program_id(1) == 0)
    def _():
        x = x_ref[...]
        y = x * lax.rsqrt(jnp.mean(x * x, axis=-1, keepdims=True) + EPS) * g_ref[...]
        for g in range(tr // group):
            rows = slice(g * group, (g + 1) * group)
            hs_ref[rows, :] = (y[rows] * (1.0 + sc_ref[g]) + sh_ref[g]).astype(BF16)
        if emit_h:
            rest[0][...] = hs_ref[...]

    o_ref[...] = jnp.dot(hs_ref[...], w_ref[...], preferred_element_type=F32)


def _norm_mod_matmul(x, gain, sc, sh, w_bf16, *, group, emit_h):
    n, d = x.shape
    nout = w_bf16.shape[1]
    tr = _pick(n, (512, 256, 128, 64))
    tn = _pick(nout, (1024, 512, 256, 128))
    gpt = tr // group
    out_shape = [jax.ShapeDtypeStruct((n, nout), F32)]
    out_specs = [pl.BlockSpec((tr, tn), lambda i, j: (i, j))]
    if emit_h:
        out_shape.append(jax.ShapeDtypeStruct((n, d), BF16))
        out_specs.append(pl.BlockSpec((tr, d), lambda i, j: (i, 0)))
    res = pl.pallas_call(
        functools.partial(_nmm_kernel, group=group, emit_h=emit_h),
        grid=(n // tr, nout // tn),
        in_specs=[
            pl.BlockSpec((tr, d), lambda i, j: (i, 0)),
            pl.BlockSpec((1, d), lambda i, j: (0, 0)),
            pl.BlockSpec((gpt, 1, d), lambda i, j: (i, 0, 0)),
            pl.BlockSpec((gpt, 1, d), lambda i, j: (i, 0, 0)),
            pl.BlockSpec((d, tn), lambda i, j: (0, j)),
        ],
        out_specs=out_specs,
        out_shape=out_shape,
        scratch_shapes=[pltpu.VMEM((tr, d), BF16)],
        name="norm_mod_matmul",
        compiler_params=_params("parallel", "arbitrary"),
    )(x, gain.reshape(1, d), sc, sh, w_bf16)
    return res if emit_h else res[0]


def _strict_upper(n):
    return (lax.broadcasted_iota(jnp.int32, (n, n), 0) > lax.broadcasted_iota(jnp.int32, (n, n), 1)).astype(BF16)


def _sb_block(q, k, v, tri, carry, acc, *, masked, scale):
    z = lax.dot_general(q, k, (((1,), (1,)), ((), ())), preferred_element_type=F32) * scale
    ls = jnp.minimum(z, 0.0) - jnp.log1p(jnp.exp(-jnp.abs(z)))
    lr = ls - z
    if masked:
        vis = (lax.broadcasted_iota(jnp.int32, z.shape, 1) < lax.broadcasted_iota(jnp.int32, z.shape, 0))
        lr = jnp.where(vis, lr, 0.0)
    hi = lr.astype(BF16)
    r1 = lr - hi.astype(F32)
    mid = r1.astype(BF16)
    lo = (r1 - mid.astype(F32)).astype(BF16)
    suf = (jnp.dot(hi, tri, preferred_element_type=F32) + jnp.dot(mid, tri, preferred_element_type=F32)
           + jnp.dot(lo, tri, preferred_element_type=F32))
    w = jnp.exp(ls + suf + carry)
    if masked:
        w = jnp.where(vis, w, 0.0)
    acc = acc + jnp.dot(w.astype(BF16), v, preferred_element_type=F32)
    carry = carry + jnp.sum(lr, axis=1, keepdims=True)
    return carry, acc


def _attn_prompt_kernel(q_ref, k_ref, v_ref, o_ref, *, heads, scale):
    tq = q_ref.shape[0]
    qi = pl.program_id(1)
    tri = _strict_upper(tq)
    for h in range(heads):
        cols = slice(h * LANES, (h + 1) * LANES)
        q = q_ref[:, cols].astype(BF16)

        def kv(kb, cols=cols):
            rows = pl.ds(pl.multiple_of(kb * tq, tq), tq)
            return k_ref[rows, cols].astype(BF16), v_ref[rows, cols].astype(BF16)

        k, v = kv(qi)
        carry, acc = _sb_block(q, k, v, tri, jnp.zeros((tq, 1), F32), jnp.zeros((tq, LANES), F32),
                               masked=True, scale=scale)

        def cond(st):
            kb, carry, _ = st
            return jnp.logical_and(kb >= 0, jnp.max(carry) > EXP_UNDERFLOW)

        def body(st, q=q, kv=kv):
            kb, carry, acc = st
            k, v = kv(kb)
            carry, acc = _sb_block(q, k, v, tri, carry, acc, masked=False, scale=scale)
            return kb - 1, carry, acc

        _, _, acc = lax.while_loop(cond, body, (qi - 1, carry, acc))
        o_ref[:, cols] = acc


def _attn_prompt(proj, *, batch, seq, heads, scale):
    w = heads * LANES
    tq = _pick(seq, (128, 64))
    nq = seq // tq
    return pl.pallas_call(
        functools.partial(_attn_prompt_kernel, heads=heads, scale=scale),
        grid=(batch, nq),
        in_specs=[
            pl.BlockSpec((tq, w), lambda b, i: (b * nq + i, 0)),
            pl.BlockSpec((seq, w), lambda b, i: (b, 1)),
            pl.BlockSpec((seq, w), lambda b, i: (b, 2)),
        ],
        out_specs=pl.BlockSpec((tq, w), lambda b, i: (b * nq + i, 0)),
        out_shape=jax.ShapeDtypeStruct((batch * seq, w), F32),
        name="attn_prompt",
        compiler_params=_params("parallel", "parallel"),
    )(proj, proj, proj)


def _attn_sample_kernel(q_ref, kn_ref, vn_ref, kc_ref, vc_ref, o_ref, cmax_ref, carry_ref, acc_ref,
                        *, heads, tk, scale):
    tq = q_ref.shape[0]
    tkp = kc_ref.shape[0]
    j = pl.program_id(1)

    def hs(h):
        return slice(h * LANES, (h + 1) * LANES)

    @pl.when(j == 0)
    def _():
        tri = _strict_upper(tq)
        for h in range(heads):
            carry, acc = _sb_block(q_ref[:, hs(h)].astype(BF16), kn_ref[:, hs(h)].astype(BF16),
                                   vn_ref[:, hs(h)].astype(BF16), tri,
                                   jnp.zeros((tq, 1), F32), jnp.zeros((tq, LANES), F32),
                                   masked=True, scale=scale)
            carry_ref[h] = jnp.broadcast_to(carry, (tq, LANES))
            acc_ref[:, hs(h)] = acc

    @pl.when(j > 0)
    def _():
        tri = _strict_upper(tk)
        for h in range(heads):
            for sb in reversed(range(tkp // tk)):
                carry = carry_ref[h][:, 0:1]

                @pl.when(jnp.max(carry) > EXP_UNDERFLOW)
                def _():
                    rows = slice(sb * tk, (sb + 1) * tk)
                    c2, a2 = _sb_block(q_ref[:, hs(h)].astype(BF16), kc_ref[rows, hs(h)].astype(BF16),
                                       vc_ref[rows, hs(h)].astype(BF16), tri, carry, acc_ref[:, hs(h)],
                                       masked=False, scale=scale)
                    carry_ref[h] = jnp.broadcast_to(c2, (tq, LANES))
                    acc_ref[:, hs(h)] = a2

    @pl.when(j == pl.num_programs(1) - 1)
    def _():
        o_ref[...] = acc_ref[...]
        for h in range(heads):
            cmax_ref[h:h + 1, :] = jnp.max(carry_ref[h], axis=0, keepdims=True)


SAMPLE_CACHE_BLOCK = 512


def _attn_sample(proj, cache_k, cache_v, *, n_prompt, seq, heads, scale):
    batch, past, w = cache_k.shape
    tkp = _pick(past, (SAMPLE_CACHE_BLOCK, 256, 128))
    tk = min(tkp, 256)
    n_past = past // tkp
    row0 = n_prompt // seq
    cache_spec = pl.BlockSpec((None, tkp, w), lambda b, j: (b, n_past - jnp.maximum(j, 1), 0))
    return pl.pallas_call(
        functools.partial(_attn_sample_kernel, heads=heads, tk=tk, scale=scale),
        grid=(batch, 1 + n_past),
        in_specs=[
            pl.BlockSpec((seq, w), lambda b, j: (row0 + b, 0)),
            pl.BlockSpec((seq, w), lambda b, j: (row0 + b, 1)),
            pl.BlockSpec((seq, w), lambda b, j: (row0 + b, 2)),
            cache_spec,
            cache_spec,
        ],
        out_specs=[
            pl.BlockSpec((seq, w), lambda b, j: (b, 0)),
            pl.BlockSpec((None, heads, LANES), lambda b, j: (b, 0, 0)),
        ],
        out_shape=[
            jax.ShapeDtypeStruct((batch * seq, w), F32),
            jax.ShapeDtypeStruct((batch, heads, LANES), F32),
        ],
        scratch_shapes=[pltpu.VMEM((heads, seq, LANES), F32), pltpu.VMEM((seq, w), F32)],
        name="attn_sample",
        compiler_params=_params("parallel", "arbitrary"),
    )(proj, proj, proj, cache_k, cache_v)


def _attn_sample_cached(proj, cache_k, cache_v, *, n_prompt, seq, heads, scale):
    batch, past = cache_k.shape[:2]
    w = heads * LANES
    kw = dict(n_prompt=n_prompt, seq=seq, heads=heads, scale=scale)
    recent = min(past, SAMPLE_CACHE_BLOCK)
    att, cmax = _attn_sample(proj, cache_k[:, past - recent:].reshape(batch, recent, w),
                             cache_v[:, past - recent:].reshape(batch, recent, w), **kw)
    if recent == past:
        return att
    return lax.cond(
        jnp.max(cmax) > EXP_UNDERFLOW,
        lambda: _attn_sample(proj, cache_k.reshape(batch, past, w), cache_v.reshape(batch, past, w), **kw)[0],
        lambda: att)


def _mix_kernel(att_ref, bg_ref, cg_ref, xc_ref, cgh_ref, xch_ref, st_ref, wc_ref, na_ref, nc_ref,
                wo_ref, x_ref, g1_ref, o_ref, tail_ref, um1_ref, um2_ref, mix_ref,
                *, group, prompt_groups, groups_per_seq):
    tr, w = att_ref.shape
    gpt = tr // group
    i = pl.program_id(0)

    u = cg_ref[...] * xc_ref[...]
    uh = cgh_ref[...] * xch_ref[...]
    rows = lax.broadcasted_iota(jnp.int32, (tr, w), 0)
    h7 = uh[SUBLANES - 1:SUBLANES, :]
    h6 = uh[SUBLANES - 2:SUBLANES - 1, :]
    um1_ref[...] = jnp.where(rows == 0, h7, pltpu.roll(u, 1, axis=0))
    um2_ref[...] = jnp.where(rows == 0, h6, jnp.where(rows == 1, h7, pltpu.roll(u, 2, axis=0)))
    r8 = lax.broadcasted_iota(jnp.int32, (SUBLANES, w), 0)
    for g in range(gpt):
        gg = i * gpt + g
        starts_sequence = jnp.logical_or(gg >= prompt_groups, gg % groups_per_seq == 0)
        head = slice(g * group, g * group + SUBLANES)

        @pl.when(starts_sequence)
        def _():
            st0 = st_ref[g, 0:1, :]
            st1 = st_ref[g, 1:2, :]
            um1_ref[head, :] = jnp.where(r8 == 0, st1, um1_ref[head, :])
            um2_ref[head, :] = jnp.where(r8 == 0, st0, jnp.where(r8 == 1, st1, um2_ref[head, :]))

        tail_ref[g] = u[(g + 1) * group - SUBLANES:(g + 1) * group, :]

    conv = wc_ref[0:1, :] * um2_ref[...] + wc_ref[1:2, :] * um1_ref[...] + wc_ref[2:3, :] * u
    cv = bg_ref[...] * conv
    att = att_ref[...]
    for k in range(w // LANES):
        cs = slice(k * LANES, (k + 1) * LANES)
        a = att[:, cs]
        mix_ref[:, cs] = (a * lax.rsqrt(jnp.mean(a * a, axis=-1, keepdims=True) + EPS)
                          * na_ref[:, cs]).astype(BF16)
        c = cv[:, cs]
        mix_ref[:, w + k * LANES:w + (k + 1) * LANES] = (
            c * lax.rsqrt(jnp.mean(c * c, axis=-1, keepdims=True) + EPS) * nc_ref[:, cs]).astype(BF16)
    y = jnp.dot(mix_ref[...], wo_ref[...], preferred_element_type=F32)
    for g in range(gpt):
        rs = slice(g * group, (g + 1) * group)
        o_ref[rs, :] = x_ref[rs, :] + g1_ref[g] * y[rs]


def _mix_out(att, proj, state, w_conv, norm_att, norm_conv, w_out_bf16, x, g1, *, group, n_prompt, seq):
    n, d = x.shape
    w = att.shape[1]
    tr = _pick(n, (256, 128, 64))
    gpt = tr // group
    hb = tr // SUBLANES
    kern = functools.partial(_mix_kernel, group=group, prompt_groups=n_prompt // group,
                             groups_per_seq=seq // group)
    halo = lambda col: pl.BlockSpec((SUBLANES, w), lambda i: (jnp.maximum(i * hb - 1, 0), col))
    return pl.pallas_call(
        kern,
        grid=(n // tr,),
        in_specs=[
            pl.BlockSpec((tr, w), lambda i: (i, 0)),
            pl.BlockSpec((tr, w), lambda i: (i, 3)),
            pl.BlockSpec((tr, w), lambda i: (i, 4)),
            pl.BlockSpec((tr, w), lambda i: (i, 5)),
            halo(4),
            halo(5),
            pl.BlockSpec((gpt, CONV_K - 1, w), lambda i: (i, 0, 0)),
            pl.BlockSpec((CONV_K, w), lambda i: (0, 0)),
            pl.BlockSpec((1, w), lambda i: (0, 0)),
            pl.BlockSpec((1, w), lambda i: (0, 0)),
            pl.BlockSpec((2 * w, d), lambda i: (0, 0)),
            pl.BlockSpec((tr, d), lambda i: (i, 0)),
            pl.BlockSpec((gpt, 1, d), lambda i: (i, 0, 0)),
        ],
        out_specs=[
            pl.BlockSpec((tr, d), lambda i: (i, 0)),
            pl.BlockSpec((gpt, SUBLANES, w), lambda i: (i, 0, 0)),
        ],
        out_shape=[
            jax.ShapeDtypeStruct((n, d), F32),
            jax.ShapeDtypeStruct((n // group, SUBLANES, w), F32),
        ],
        scratch_shapes=[pltpu.VMEM((tr, w), F32), pltpu.VMEM((tr, w), F32), pltpu.VMEM((tr, 2 * w), BF16)],
        name="mix_out",
        compiler_params=_params("parallel"),
    )(att, proj, proj, proj, proj, proj, state, w_conv, norm_att.reshape(1, w), norm_conv.reshape(1, w),
      w_out_bf16, x, g1)


def _staircase():
    return [(r, c) for r in range(PEER_TOPK) for c in range(PEER_TOPK) if (r + 1) * (c + 1) <= PEER_TOPK]


def _sort16_network():
    pairs, n, p = [], 16, 1
    while p < n:
        k = p
        while k >= 1:
            for j in range(k % p, n - k, 2 * k):
                for i in range(min(k, n - j - k)):
                    if (i + j) // (2 * p) == (i + j + k) // (2 * p):
                        pairs.append((i + j, i + j + k))
            k //= 2
        p *= 2
    return pairs


def _top16_values(slabs):
    x = list(slabs)
    for i, j in _sort16_network():
        x[i], x[j] = jnp.maximum(x[i], x[j]), jnp.minimum(x[i], x[j])
    for shift in (4, 2, 1):
        z = [jnp.maximum(x[r], pltpu.roll(x[PEER_TOPK - 1 - r], shift, axis=0)) for r in range(PEER_TOPK)]
        for dist in (8, 4, 2, 1):
            for i in range(PEER_TOPK):
                if i & dist == 0:
                    z[i], z[i + dist] = jnp.maximum(z[i], z[i + dist]), jnp.minimum(z[i], z[i + dist])
        x = z
    return x


def _top16_ranked(s, a_ref, slot):
    nk, tt = s.shape
    kidx = lax.broadcasted_iota(jnp.int32, (nk, tt), 0).astype(F32)
    work = s
    rank = jnp.full((nk, tt), float(PEER_TOPK), F32)
    for r in range(PEER_TOPK):
        m = jnp.max(work, axis=0, keepdims=True)
        first = jnp.min(jnp.where(work == m, kidx, float(nk)), axis=0, keepdims=True)
        hit = kidx == first
        rank = jnp.where(hit, float(r), rank)
        work = jnp.where(hit, -jnp.inf, work)
        a_ref[r, pl.ds(slot, 1), :] = m
    return rank


def _kept_per_row(a_ref, l_ref, heads):
    cands = _staircase()
    a0 = [a_ref[r, 0:heads, :] for r in range(PEER_TOPK)]
    a1 = [a_ref[c, heads:2 * heads, :] for c in range(PEER_TOPK)]
    sums = [a0[r] + a1[c] for r, c in cands]
    ahead = [0.0] * len(cands)
    for xi, (rx, cx) in enumerate(cands):
        for yi in range(xi):
            ry, cy = cands[yi]
            if ry <= rx and cy <= cx:
                ahead[xi] = ahead[xi] + 1.0
            else:
                y_first = sums[yi] >= sums[xi]
                ahead[xi] = ahead[xi] + jnp.where(y_first, 1.0, 0.0)
                ahead[yi] = ahead[yi] + jnp.where(y_first, 0.0, 1.0)
    zsum = jnp.zeros_like(sums[0])
    kept = [jnp.zeros_like(sums[0]) for _ in range(PEER_TOPK)]
    for xi, (rx, cx) in enumerate(cands):
        keep = jnp.where(ahead[xi] < float(PEER_TOPK), 1.0, 0.0)
        kept[rx] = kept[rx] + keep
        zsum = zsum + keep * jnp.exp(sums[xi] - sums[0])
    for r in range(PEER_TOPK):
        l_ref[r] = kept[r]
    l_ref[PEER_TOPK] = 1.0 / zsum


def _peer_select_kernel(q_ref, key_ref, ea_ref, lr_ref, eb_ref, cr_ref, s_ref, rk_ref, a_ref, l_ref,
                        *, heads):
    nk, tt = s_ref.shape[1], s_ref.shape[2]
    nslab = nk // SUBLANES
    ambiguous = jnp.zeros((1, tt), F32)
    for hp in range(2 * heads):
        slot = (hp % 2) * heads + hp // 2
        s = lax.dot_general(key_ref[hp], q_ref[:, hp * LANES:(hp + 1) * LANES], (((1,), (1,)), ((), ())),
                            precision=lax.Precision.HIGHEST, preferred_element_type=F32)
        s_ref[slot] = s
        slabs = [s[v * SUBLANES:(v + 1) * SUBLANES, :] for v in range(nslab)]
        best = _top16_values(slabs)
        for r in range(PEER_TOPK):
            a_ref[r, slot:slot + 1, :] = best[r][0:1, :]
        at_least_last = jnp.zeros((SUBLANES, tt), F32)
        for v in range(nslab):
            at_least_last = at_least_last + jnp.where(slabs[v] >= best[PEER_TOPK - 1], 1.0, 0.0)
        bad = jnp.sum(at_least_last, axis=0, keepdims=True) != float(PEER_TOPK)
        for r in range(PEER_TOPK - 1):
            bad = jnp.logical_or(bad, best[r][0:1, :] == best[r + 1][0:1, :])
        ambiguous = jnp.maximum(ambiguous, jnp.where(bad, 1.0, 0.0))
    redo = jnp.max(ambiguous) > 0.0

    @pl.when(redo)
    def _():
        def one_slot(slot, carry):
            rk_ref[slot] = _top16_ranked(s_ref[slot], a_ref, slot)
            return carry
        lax.fori_loop(0, 2 * heads, one_slot, 0)

    _kept_per_row(a_ref, l_ref, heads)

    for h in range(heads):
        ea_ref[h] = jnp.exp(s_ref[h] - a_ref[0, h:h + 1, :]) * l_ref[PEER_TOPK, h:h + 1, :]
        eb_ref[h] = jnp.exp(s_ref[heads + h] - a_ref[0, heads + h:heads + h + 1, :])

    @pl.when(jnp.logical_not(redo))
    def _():
        for h in range(heads):
            s0, s1 = s_ref[h], s_ref[heads + h]
            lrow = jnp.zeros((nk, tt), F32)
            crank = jnp.full((nk, tt), float(PEER_TOPK), F32)
            for r in range(PEER_TOPK):
                lrow = jnp.where(s0 == a_ref[r, h:h + 1, :], l_ref[r, h:h + 1, :], lrow)
                crank = jnp.where(s1 == a_ref[r, heads + h:heads + h + 1, :], float(r), crank)
            lr_ref[h] = lrow
            cr_ref[h] = crank

    @pl.when(redo)
    def _():
        for h in range(heads):
            rk0 = rk_ref[h]
            lrow = jnp.zeros((nk, tt), F32)
            for r in range(PEER_TOPK):
                lrow = jnp.where(rk0 == float(r), l_ref[r, h:h + 1, :], lrow)
            lr_ref[h] = lrow
            cr_ref[h] = rk_ref[heads + h]


def _peer_select(q, keys):
    n, dq = q.shape
    heads, _, nk, dk = keys.shape
    assert nk == PEER_TOPK * SUBLANES and dk == LANES
    tt = _pick(n, (256, 128))
    out = jax.ShapeDtypeStruct((heads, nk, n), F32)
    ospec = pl.BlockSpec((heads, nk, tt), lambda t: (0, 0, t))
    return pl.pallas_call(
        functools.partial(_peer_select_kernel, heads=heads),
        grid=(n // tt,),
        in_specs=[
            pl.BlockSpec((tt, dq), lambda t: (t, 0)),
            pl.BlockSpec((2 * heads, nk, dk), lambda t: (0, 0, 0)),
        ],
        out_specs=[ospec, ospec, ospec, ospec],
        out_shape=[out, out, out, out],
        scratch_shapes=[
            pltpu.VMEM((2 * heads, nk, tt), F32),
            pltpu.VMEM((2 * heads, nk, tt), F32),
            pltpu.VMEM((PEER_TOPK, 2 * heads, tt), F32),
            pltpu.VMEM((PEER_TOPK + 1, heads, tt), F32),
        ],
        name="peer_select",
        compiler_params=_params("parallel"),
    )(q, keys.reshape(2 * heads, nk, dk))


def _peer_dense_kernel(h_ref, u_ref, vt_ref, ea_ref, lr_ref, eb_ref, cr_ref, o_ref, acc_ref, act0_ref, act1_ref,
                       coef_ref, *, heads):
    ec, tt = act0_ref.shape
    c = pl.program_id(1)
    last = pl.num_programs(1) - 1

    @pl.when(jnp.logical_and(pl.program_id(0) == 0, c == 0))
    def _():
        act1_ref[...] = jnp.zeros_like(act1_ref)
        acc_ref[...] = jnp.zeros_like(acc_ref)

    def step(act_w, act_r):
        act_w[...] = lax.dot_general(u_ref[...], h_ref[...], (((1,), (1,)), ((), ())),
                                     preferred_element_type=F32)
        half = ec // 2
        partial = []
        for kh in range(2):
            for il in range(kh * half // LANES, (kh + 1) * half // LANES):
                rows = slice(il * LANES, (il + 1) * LANES)
                for tb in range(tt // LANES):
                    cols = slice(tb * LANES, (tb + 1) * LANES)
                    wsum = jnp.zeros((LANES, LANES), F32)
                    for h in range(heads):
                        keep = cr_ref[h, :, cols] < lr_ref[h, il:il + 1, cols]
                        wsum = wsum + jnp.where(keep, eb_ref[h, :, cols] * ea_ref[h, il:il + 1, cols], 0.0)
                    g = jax.nn.gelu(act_r[rows, cols], approximate=True)
                    coef_ref[rows, cols] = (wsum * g).astype(BF16)
            ks = slice(kh * half, (kh + 1) * half)
            partial.append(jnp.dot(vt_ref[:, ks], coef_ref[ks, :], preferred_element_type=F32))
        acc_ref[...] += partial[0] + partial[1]

    @pl.when(c % 2 == 0)
    def _():
        step(act0_ref, act1_ref)

    @pl.when(c % 2 == 1)
    def _():
        step(act1_ref, act0_ref)

    @pl.when(c == 0)
    def _():
        acc_ref[...] = jnp.zeros_like(acc_ref)

    @pl.when(c == last)
    def _():
        o_ref[...] = acc_ref[...].T


def _peer_dense(h_bf16, u_bf16, vt_bf16, ea, lr, eb, cr):
    n, d = h_bf16.shape
    e = u_bf16.shape[0]
    heads, nk, _ = ea.shape
    tt = _pick(n, (512, 256, 128))
    ec = _pick(e, (1024,))
    nc = e // ec
    ipc = ec // LANES
    sel_i = pl.BlockSpec((heads, ipc, tt), lambda t, c: (0, jnp.maximum(c - 1, 0), t))
    sel_j = pl.BlockSpec((heads, nk, tt), lambda t, c: (0, 0, t))
    return pl.pallas_call(
        functools.partial(_peer_dense_kernel, heads=heads),
        grid=(n // tt, nc + 1),
        in_specs=[
            pl.BlockSpec((tt, d), lambda t, c: (t, 0)),
            pl.BlockSpec((ec, d), lambda t, c: (jnp.minimum(c, nc - 1), 0)),
            pl.BlockSpec((d, ec), lambda t, c: (0, jnp.maximum(c - 1, 0))),
            sel_i, sel_i, sel_j, sel_j,
        ],
        out_specs=pl.BlockSpec((tt, d), lambda t, c: (t, 0)),
        out_shape=jax.ShapeDtypeStruct((n, d), F32),
        scratch_shapes=[pltpu.VMEM((d, tt), F32), pltpu.VMEM((ec, tt), F32), pltpu.VMEM((ec, tt), F32),
                        pltpu.VMEM((ec, tt), BF16)],
        name="peer_dense",
        compiler_params=_params("arbitrary", "arbitrary"),
    )(h_bf16, u_bf16, vt_bf16, ea, lr, eb, cr)


def _resid_kernel(x_ref, y_ref, g_ref, o_ref, *, group):
    for g in range(x_ref.shape[0] // group):
        rs = slice(g * group, (g + 1) * group)
        o_ref[rs, :] = x_ref[rs, :] + g_ref[g] * y_ref[rs, :]


def _gated_residual(x, y, gate, *, group):
    n, d = x.shape
    tr = _pick(n, (512, 256, 128, 64))
    row = pl.BlockSpec((tr, d), lambda i: (i, 0))
    return pl.pallas_call(
        functools.partial(_resid_kernel, group=group),
        grid=(n // tr,),
        in_specs=[row, row, pl.BlockSpec((tr // group, 1, d), lambda i: (i, 0, 0))],
        out_specs=row,
        out_shape=jax.ShapeDtypeStruct((n, d), F32),
        compiler_params=_params("parallel"),
    )(x, y, gate)


def _final_norm_kernel(x_ref, g_ref, o_ref):
    x = x_ref[...]
    o_ref[...] = x * lax.rsqrt(jnp.mean(x * x, axis=-1, keepdims=True) + EPS) * g_ref[...]


def _final_norm(x, gain):
    n, d = x.shape
    tr = _pick(n, (512, 256, 128, 64))
    row = pl.BlockSpec((tr, d), lambda i: (i, 0))
    return pl.pallas_call(
        _final_norm_kernel,
        grid=(n // tr,),
        in_specs=[row, pl.BlockSpec((1, d), lambda i: (0, 0))],
        out_specs=row,
        out_shape=jax.ShapeDtypeStruct((n, d), F32),
        compiler_params=_params("parallel"),
    )(x, gain.reshape(1, d))


def kernel(x_prompt, x_sample, cache_k, cache_v, state_conv, c_prompt, c_sample, w_ada, b_ada, norm_mix,
           w_in, w_conv, norm_att_out, norm_conv_out, w_out, norm_ffn, w_peer_q, peer_keys, peer_u, peer_v,
           norm_final):
    bp, tp, d = x_prompt.shape
    bs, ts, _ = x_sample.shape
    depth = w_in.shape[0]
    past = cache_k.shape[2]
    heads = cache_k.shape[3]
    w = heads * LANES
    assert cache_k.shape[4] == LANES and state_conv.shape[-1] == w and w_in.shape[2] == 6 * w
    assert tp % ts == 0 and ts % SUBLANES == 0
    group = ts
    n_p, n_s = bp * tp, bs * ts
    n = n_p + n_s
    scale = float(LANES) ** -0.5

    x = jnp.concatenate([x_prompt.reshape(n_p, d), x_sample.reshape(n_s, d)], axis=0)

    c_all = jnp.concatenate([c_prompt, c_sample], axis=0)
    r = c_all.shape[0]
    r_pad = -(-r // SUBLANES) * SUBLANES
    mod = _ada(jnp.pad(c_all, ((0, r_pad - r), (0, 0))), w_ada, b_ada)
    group_row = jnp.concatenate([jnp.repeat(jnp.arange(bp), tp // group), bp + jnp.arange(bs)])
    mod = jnp.take(mod, group_row, axis=1).reshape(depth, n // group, 6, 1, d)

    state =jnp.concatenate([jnp.zeros((depth, n_p // group, CONV_K - 1, w), F32), state_conv], axis=1)
    seq_end_group = jnp.concatenate([(jnp.arange(bp) + 1) * (tp // group) - 1, n_p // group + jnp.arange(bs)])

    kp, vp, cp, ks, vs, cs = [], [], [], [], [], []
    for l in range(depth):
        sh1, sc1, g1, sh2, sc2, g2 = (mod[l, :, k] for k in range(6))
        proj = _norm_mod_matmul(x, norm_mix[l], sc1, sh1, w_in[l].astype(BF16), group=group, emit_h=False)
        att_p = _attn_prompt(proj, batch=bp, seq=tp, heads=heads, scale=scale)
        att_s = _attn_sample_cached(proj, cache_k[l], cache_v[l], n_prompt=n_p, seq=ts, heads=heads, scale=scale)
        att = jnp.concatenate([att_p, att_s], axis=0)
        x, tails = _mix_out(att, proj, state[l], w_conv[l], norm_att_out[l], norm_conv_out[l],
                            w_out[l].astype(BF16), x, g1, group=group, n_prompt=n_p, seq=tp)
        q, h2 = _norm_mod_matmul(x, norm_ffn[l], sc2, sh2, w_peer_q[l].astype(BF16), group=group, emit_h=True)
        ea, lr, eb, cr = _peer_select(q, peer_keys[l])
        y = _peer_dense(h2, peer_u[l].astype(BF16), peer_v[l].astype(BF16).T, ea, lr, eb, cr)
        x = _gated_residual(x, y, g2, group=group)

        k_new, v_new = proj[:, w:2 * w], proj[:, 2 * w:3 * w]
        conv_new = jnp.take(tails, seq_end_group, axis=0)[:, SUBLANES - (CONV_K - 1):, :]
        kp.append(k_new[:n_p].reshape(bp, tp, heads, LANES))
        vp.append(v_new[:n_p].reshape(bp, tp, heads, LANES))
        ks.append(k_new[n_p:].reshape(bs, ts, heads, LANES))
        vs.append(v_new[n_p:].reshape(bs, ts, heads, LANES))
        cp.append(conv_new[:bp])
        cs.append(conv_new[bp:])

    y = _final_norm(x, norm_final)
    return (y[:n_p].reshape(bp, tp, d), y[n_p:].reshape(bs, ts, d), jnp.stack(kp), jnp.stack(vp),
            jnp.stack(cp), jnp.stack(ks), jnp.stack(vs), jnp.stack(cs))
```

```python
import functools
import math

import jax
import jax.numpy as jnp
from jax import lax
from jax.experimental import pallas as pl
from jax.experimental.pallas import tpu as pltpu

F32 = jnp.float32
BF16 = jnp.bfloat16
EPS = 1e-6
LANES = 128
SUBLANES = 8
CONV_K = 3
PEER_TOPK = 16
VMEM_LIMIT_BYTES = 56 * 1024 * 1024
EXP_UNDERFLOW = -104.0


def _params(*sem):
    return pltpu.CompilerParams(dimension_semantics=sem, vmem_limit_bytes=VMEM_LIMIT_BYTES)


def _pick(n, cands):
    for c in cands:
        if n % c == 0:
            return c
    raise ValueError(f"no tile in {cands} divides {n}")


def _ada_kernel(c_ref, w_ref, b_ref, o_ref):
    c = c_ref[...]
    s = (c * jax.nn.sigmoid(c)).astype(BF16)
    o_ref[...] = jnp.dot(s, w_ref[...].astype(BF16), preferred_element_type=F32) + b_ref[...]


def _ada(c_all, w_ada, b_ada):
    depth, d, n6 = w_ada.shape
    r = c_all.shape[0]
    tn = _pick(n6, (1024, 512, 256, 128))
    return pl.pallas_call(
        _ada_kernel,
        grid=(depth, n6 // tn),
        in_specs=[
            pl.BlockSpec((r, d), lambda l, j: (0, 0)),
            pl.BlockSpec((None, d, tn), lambda l, j: (l, 0, j)),
            pl.BlockSpec((None, 1, tn), lambda l, j: (l, 0, j)),
        ],
        out_specs=pl.BlockSpec((None, r, tn), lambda l, j: (l, 0, j)),
        out_shape=jax.ShapeDtypeStruct((depth, r, n6), F32),
        name="ada",
        compiler_params=_params("parallel", "parallel"),
    )(c_all, w_ada, b_ada.reshape(depth, 1, n6))


def _nmm_kernel(x_ref, g_ref, sc_ref, sh_ref, w_ref, o_ref, *rest, group, emit_h):
    hs_ref = rest[-1]
    tr = x_ref.shape[0]

    @pl.when(pl.program_id(1) == 0)
    def _():
        x = x_ref[...]
        y = x * lax.rsqrt(jnp.mean(x * x, axis=-1, keepdims=True) + EPS) * g_ref[...]
        for g in range(tr // group):
            rows = slice(g * group, (g + 1) * group)
            hs_ref[rows, :] = (y[rows] * (1.0 + sc_ref[g]) + sh_ref[g]).astype(BF16)
        if emit_h:
            rest[0][...] = hs_ref[...]

    o_ref[...] = jnp.dot(hs_ref[...], w_ref[...], preferred_element_type=F32)


def _norm_mod_matmul(x, gain, sc, sh, w_bf16, *, group, emit_h):
    n, d = x.shape
    nout = w_bf16.shape[1]
    tr = _pick(n, (512, 256, 128, 64))
    tn = _pick(nout, (1024, 512, 256, 128))
    gpt = tr // group
    out_shape = [jax.ShapeDtypeStruct((n, nout), F32)]
    out_specs = [pl.BlockSpec((tr, tn), lambda i, j: (i, j))]
    if emit_h:
        out_shape.append(jax.ShapeDtypeStruct((n, d), BF16))
        out_specs.append(pl.BlockSpec((tr, d), lambda i, j: (i, 0)))
    res = pl.pallas_call(
        functools.partial(_nmm_kernel, group=group, emit_h=emit_h),
        grid=(n // tr, nout // tn),
        in_specs=[
            pl.BlockSpec((tr, d), lambda i, j: (i, 0)),
            pl.BlockSpec((1, d), lambda i, j: (0, 0)),
            pl.BlockSpec((gpt, 1, d), lambda i, j: (i, 0, 0)),
            pl.BlockSpec((gpt, 1, d), lambda i, j: (i, 0, 0)),
            pl.BlockSpec((d, tn), lambda i, j: (0, j)),
        ],
        out_specs=out_specs,
        out_shape=out_shape,
        scratch_shapes=[pltpu.VMEM((tr, d), BF16)],
        name="norm_mod_matmul",
        compiler_params=_params("parallel", "arbitrary"),
    )(x, gain.reshape(1, d), sc, sh, w_bf16)
    return res if emit_h else res[0]


def _strict_upper(n):
    return (lax.broadcasted_iota(jnp.int32, (n, n), 0) > lax.broadcasted_iota(jnp.int32, (n, n), 1)).astype(BF16)


def _sb_block(q, k, v, tri, carry, acc, *, masked, scale):
    z = lax.dot_general(q, k, (((1,), (1,)), ((), ())), preferred_element_type=F32) * scale
    ls = jnp.minimum(z, 0.0) - jnp.log1p(jnp.exp(-jnp.abs(z)))
    lr = ls - z
    if masked:
        vis = (lax.broadcasted_iota(jnp.int32, z.shape, 1) < lax.broadcasted_iota(jnp.int32, z.shape, 0))
        lr = jnp.where(vis, lr, 0.0)
    hi = lr.astype(BF16)
    r1 = lr - hi.astype(F32)
    mid = r1.astype(BF16)
    lo = (r1 - mid.astype(F32)).astype(BF16)
    suf = (jnp.dot(hi, tri, preferred_element_type=F32) + jnp.dot(mid, tri, preferred_element_type=F32)
           + jnp.dot(lo, tri, preferred_element_type=F32))
    w = jnp.exp(ls + suf + carry)
    if masked:
        w = jnp.where(vis, w, 0.0)
    acc = acc + jnp.dot(w.astype(BF16), v, preferred_element_type=F32)
    carry = carry + jnp.sum(lr, axis=1, keepdims=True)
    return carry, acc


def _attn_prompt_kernel(q_ref, k_ref, v_ref, o_ref, *, heads, scale):
    tq = q_ref.shape[0]
    qi = pl.program_id(1)
    tri = _strict_upper(tq)
    cols = [slice(h * LANES, (h + 1) * LANES) for h in range(heads)]
    qs = [q_ref[:, cols[h]].astype(BF16) for h in range(heads)]

    def all_heads(kb, carries, accs, masked):
        rows = pl.ds(pl.multiple_of(kb * tq, tq), tq)
        out = [_sb_block(qs[h], k_ref[rows, cols[h]].astype(BF16), v_ref[rows, cols[h]].astype(BF16), tri,
                         carries[h], accs[h], masked=masked, scale=scale) for h in range(heads)]
        return [c for c, _ in out], [a for _, a in out]

    carries, accs = all_heads(qi, [jnp.zeros((tq, 1), F32)] * heads, [jnp.zeros((tq, LANES), F32)] * heads, True)

    def cond(st):
        kb, carries, _ = st
        live = functools.reduce(jnp.maximum, carries)
        return jnp.logical_and(kb >= 0, jnp.max(live) > EXP_UNDERFLOW)

    def body(st):
        kb, carries, accs = st
        carries, accs = all_heads(kb, carries, accs, False)
        return kb - 1, carries, accs

    _, _, accs = lax.while_loop(cond, body, (qi - 1, carries, accs))
    for h in range(heads):
        o_ref[:, cols[h]] = accs[h]


def _attn_prompt(proj, *, batch, seq, heads, scale):
    w = heads * LANES
    tq = _pick(seq, (128, 64))
    nq = seq // tq
    return pl.pallas_call(
        functools.partial(_attn_prompt_kernel, heads=heads, scale=scale),
        grid=(batch, nq),
        in_specs=[
            pl.BlockSpec((tq, w), lambda b, i: (b * nq + i, 0)),
            pl.BlockSpec((seq, w), lambda b, i: (b, 1)),
            pl.BlockSpec((seq, w), lambda b, i: (b, 2)),
        ],
        out_specs=pl.BlockSpec((tq, w), lambda b, i: (b * nq + i, 0)),
        out_shape=jax.ShapeDtypeStruct((batch * seq, w), F32),
        name="attn_prompt",
        compiler_params=_params("parallel", "parallel"),
    )(proj, proj, proj)


def _attn_sample_kernel(q_ref, kn_ref, vn_ref, kc_ref, vc_ref, o_ref, cmax_ref, carry_ref, acc_ref,
                        *, heads, tk, scale):
    tq = q_ref.shape[0]
    tkp = kc_ref.shape[0]
    j = pl.program_id(1)

    def hs(h):
        return slice(h * LANES, (h + 1) * LANES)

    @pl.when(j == 0)
    def _():
        tri = _strict_upper(tq)
        for h in range(heads):
            carry, acc = _sb_block(q_ref[:, hs(h)].astype(BF16), kn_ref[:, hs(h)].astype(BF16),
                                   vn_ref[:, hs(h)].astype(BF16), tri,
                                   jnp.zeros((tq, 1), F32), jnp.zeros((tq, LANES), F32),
                                   masked=True, scale=scale)
            carry_ref[h] = jnp.broadcast_to(carry, (tq, LANES))
            acc_ref[:, hs(h)] = acc

    @pl.when(j > 0)
    def _():
        tri = _strict_upper(tk)
        for h in range(heads):
            for sb in reversed(range(tkp // tk)):
                carry = carry_ref[h][:, 0:1]

                @pl.when(jnp.max(carry) > EXP_UNDERFLOW)
                def _():
                    rows = slice(sb * tk, (sb + 1) * tk)
                    c2, a2 = _sb_block(q_ref[:, hs(h)].astype(BF16), kc_ref[rows, hs(h)].astype(BF16),
                                       vc_ref[rows, hs(h)].astype(BF16), tri, carry, acc_ref[:, hs(h)],
                                       masked=False, scale=scale)
                    carry_ref[h] = jnp.broadcast_to(c2, (tq, LANES))
                    acc_ref[:, hs(h)] = a2

    @pl.when(j == pl.num_programs(1) - 1)
    def _():
        o_ref[...] = acc_ref[...]
        for h in range(heads):
            cmax_ref[h:h + 1, :] = jnp.max(carry_ref[h], axis=0, keepdims=True)


SAMPLE_CACHE_BLOCK = 512


def _attn_sample(proj, cache_k, cache_v, *, n_prompt, seq, heads, scale):
    batch, past, w = cache_k.shape
    tkp = _pick(past, (SAMPLE_CACHE_BLOCK, 256, 128))
    tk = min(tkp, 256)
    n_past = past // tkp
    row0 = n_prompt // seq
    cache_spec = pl.BlockSpec((None, tkp, w), lambda b, j: (b, n_past - jnp.maximum(j, 1), 0))
    return pl.pallas_call(
        functools.partial(_attn_sample_kernel, heads=heads, tk=tk, scale=scale),
        grid=(batch, 1 + n_past),
        in_specs=[
            pl.BlockSpec((seq, w), lambda b, j: (row0 + b, 0)),
            pl.BlockSpec((seq, w), lambda b, j: (row0 + b, 1)),
            pl.BlockSpec((seq, w), lambda b, j: (row0 + b, 2)),
            cache_spec,
            cache_spec,
        ],
        out_specs=[
            pl.BlockSpec((seq, w), lambda b, j: (b, 0)),
            pl.BlockSpec((None, heads, LANES), lambda b, j: (b, 0, 0)),
        ],
        out_shape=[
            jax.ShapeDtypeStruct((batch * seq, w), F32),
            jax.ShapeDtypeStruct((batch, heads, LANES), F32),
        ],
        scratch_shapes=[pltpu.VMEM((heads, seq, LANES), F32), pltpu.VMEM((seq, w), F32)],
        name="attn_sample",
        compiler_params=_params("parallel", "arbitrary"),
    )(proj, proj, proj, cache_k, cache_v)


def _attn_sample_cached(proj, cache_k, cache_v, layer, *, n_prompt, seq, heads, scale):
    batch, past = cache_k.shape[1:3]
    w = heads * LANES
    kw = dict(n_prompt=n_prompt, seq=seq, heads=heads, scale=scale)
    recent = min(past, SAMPLE_CACHE_BLOCK)
    att, cmax = _attn_sample(proj, cache_k[layer, :, past - recent:].reshape(batch, recent, w),
                             cache_v[layer, :, past - recent:].reshape(batch, recent, w), **kw)
    if recent == past:
        return att
    return lax.cond(
        jnp.max(cmax) > EXP_UNDERFLOW,
        lambda ck, cv: _attn_sample(proj, ck[layer].reshape(batch, past, w), cv[layer].reshape(batch, past, w),
                                    **kw)[0],
        lambda ck, cv: att,
        cache_k, cache_v)


def _mix_kernel(att_ref, bg_ref, cg_ref, xc_ref, cgh_ref, xch_ref, st_ref, wc_ref, na_ref, nc_ref,
                wo_ref, x_ref, g1_ref, o_ref, tail_ref, um1_ref, um2_ref, mix_ref,
                *, group, prompt_groups, groups_per_seq):
    tr, w = att_ref.shape
    gpt = tr // group
    i = pl.program_id(0)

    u = cg_ref[...] * xc_ref[...]
    uh = cgh_ref[...] * xch_ref[...]
    rows = lax.broadcasted_iota(jnp.int32, (tr, w), 0)
    h7 = uh[SUBLANES - 1:SUBLANES, :]
    h6 = uh[SUBLANES - 2:SUBLANES - 1, :]
    um1_ref[...] = jnp.where(rows == 0, h7, pltpu.roll(u, 1, axis=0))
    um2_ref[...] = jnp.where(rows == 0, h6, jnp.where(rows == 1, h7, pltpu.roll(u, 2, axis=0)))
    r8 = lax.broadcasted_iota(jnp.int32, (SUBLANES, w), 0)
    for g in range(gpt):
        gg = i * gpt + g
        starts_sequence = jnp.logical_or(gg >= prompt_groups, gg % groups_per_seq == 0)
        head = slice(g * group, g * group + SUBLANES)

        @pl.when(starts_sequence)
        def _():
            st0 = st_ref[g, 0:1, :]
            st1 = st_ref[g, 1:2, :]
            um1_ref[head, :] = jnp.where(r8 == 0, st1, um1_ref[head, :])
            um2_ref[head, :] = jnp.where(r8 == 0, st0, jnp.where(r8 == 1, st1, um2_ref[head, :]))

        tail_ref[g] = u[(g + 1) * group - SUBLANES:(g + 1) * group, :]

    conv = wc_ref[0:1, :] * um2_ref[...] + wc_ref[1:2, :] * um1_ref[...] + wc_ref[2:3, :] * u
    cv = bg_ref[...] * conv
    att = att_ref[...]
    for k in range(w // LANES):
        cs = slice(k * LANES, (k + 1) * LANES)
        a = att[:, cs]
        mix_ref[:, cs] = (a * lax.rsqrt(jnp.mean(a * a, axis=-1, keepdims=True) + EPS)
                          * na_ref[:, cs]).astype(BF16)
        c = cv[:, cs]
        mix_ref[:, w + k * LANES:w + (k + 1) * LANES] = (
            c * lax.rsqrt(jnp.mean(c * c, axis=-1, keepdims=True) + EPS) * nc_ref[:, cs]).astype(BF16)
    y = jnp.dot(mix_ref[...], wo_ref[...], preferred_element_type=F32)
    for g in range(gpt):
        rs = slice(g * group, (g + 1) * group)
        o_ref[rs, :] = x_ref[rs, :] + g1_ref[g] * y[rs]


def _mix_out(att, proj, state, w_conv, norm_att, norm_conv, w_out_bf16, x, g1, *, group, n_prompt, seq):
    n, d = x.shape
    w = att.shape[1]
    tr = _pick(n, (256, 128, 64))
    gpt = tr // group
    hb = tr // SUBLANES
    kern = functools.partial(_mix_kernel, group=group, prompt_groups=n_prompt // group,
                             groups_per_seq=seq // group)
    halo = lambda col: pl.BlockSpec((SUBLANES, w), lambda i: (jnp.maximum(i * hb - 1, 0), col))
    return pl.pallas_call(
        kern,
        grid=(n // tr,),
        in_specs=[
            pl.BlockSpec((tr, w), lambda i: (i, 0)),
            pl.BlockSpec((tr, w), lambda i: (i, 3)),
            pl.BlockSpec((tr, w), lambda i: (i, 4)),
            pl.BlockSpec((tr, w), lambda i: (i, 5)),
            halo(4),
            halo(5),
            pl.BlockSpec((gpt, CONV_K - 1, w), lambda i: (i, 0, 0)),
            pl.BlockSpec((CONV_K, w), lambda i: (0, 0)),
            pl.BlockSpec((1, w), lambda i: (0, 0)),
            pl.BlockSpec((1, w), lambda i: (0, 0)),
            pl.BlockSpec((2 * w, d), lambda i: (0, 0)),
            pl.BlockSpec((tr, d), lambda i: (i, 0)),
            pl.BlockSpec((gpt, 1, d), lambda i: (i, 0, 0)),
        ],
        out_specs=[
            pl.BlockSpec((tr, d), lambda i: (i, 0)),
            pl.BlockSpec((gpt, SUBLANES, w), lambda i: (i, 0, 0)),
        ],
        out_shape=[
            jax.ShapeDtypeStruct((n, d), F32),
            jax.ShapeDtypeStruct((n // group, SUBLANES, w), F32),
        ],
        scratch_shapes=[pltpu.VMEM((tr, w), F32), pltpu.VMEM((tr, w), F32), pltpu.VMEM((tr, 2 * w), BF16)],
        name="mix_out",
        compiler_params=_params("parallel"),
    )(att, proj, proj, proj, proj, proj, state, w_conv, norm_att.reshape(1, w), norm_conv.reshape(1, w),
      w_out_bf16, x, g1)


def _staircase():
    return [(r, c) for r in range(PEER_TOPK) for c in range(PEER_TOPK) if (r + 1) * (c + 1) <= PEER_TOPK]


def _sort16_network():
    pairs, n, p = [], 16, 1
    while p < n:
        k = p
        while k >= 1:
            for j in range(k % p, n - k, 2 * k):
                for i in range(min(k, n - j - k)):
                    if (i + j) // (2 * p) == (i + j + k) // (2 * p):
                        pairs.append((i + j, i + j + k))
            k //= 2
        p *= 2
    return pairs


def _top16_values(slabs):
    x = list(slabs)
    for i, j in _sort16_network():
        x[i], x[j] = jnp.maximum(x[i], x[j]), jnp.minimum(x[i], x[j])
    for shift in (4, 2, 1):
        z = [jnp.maximum(x[r], pltpu.roll(x[PEER_TOPK - 1 - r], shift, axis=0)) for r in range(PEER_TOPK)]
        for dist in (8, 4, 2, 1):
            for i in range(PEER_TOPK):
                if i & dist == 0:
                    z[i], z[i + dist] = jnp.maximum(z[i], z[i + dist]), jnp.minimum(z[i], z[i + dist])
        x = z
    return x


def _top16_ranked(s, a_ref, slot):
    nk, tt = s.shape
    kidx = lax.broadcasted_iota(jnp.int32, (nk, tt), 0).astype(F32)
    work = s
    rank = jnp.full((nk, tt), float(PEER_TOPK), F32)
    for r in range(PEER_TOPK):
        m = jnp.max(work, axis=0, keepdims=True)
        first = jnp.min(jnp.where(work == m, kidx, float(nk)), axis=0, keepdims=True)
        hit = kidx == first
        rank = jnp.where(hit, float(r), rank)
        work = jnp.where(hit, -jnp.inf, work)
        a_ref[r, pl.ds(slot, 1), :] = m
    return rank


def _kept_per_row(a_ref, l_ref, heads):
    cands = _staircase()
    a0 = [a_ref[r, 0:heads, :] for r in range(PEER_TOPK)]
    a1 = [a_ref[c, heads:2 * heads, :] for c in range(PEER_TOPK)]
    sums = [a0[r] + a1[c] for r, c in cands]
    ahead = [0.0] * len(cands)
    for xi, (rx, cx) in enumerate(cands):
        for yi in range(xi):
            ry, cy = cands[yi]
            if ry <= rx and cy <= cx:
                ahead[xi] = ahead[xi] + 1.0
            else:
                y_first = sums[yi] >= sums[xi]
                ahead[xi] = ahead[xi] + jnp.where(y_first, 1.0, 0.0)
                ahead[yi] = ahead[yi] + jnp.where(y_first, 0.0, 1.0)
    zsum = jnp.zeros_like(sums[0])
    kept = [jnp.zeros_like(sums[0]) for _ in range(PEER_TOPK)]
    for xi, (rx, cx) in enumerate(cands):
        keep = jnp.where(ahead[xi] < float(PEER_TOPK), 1.0, 0.0)
        kept[rx] = kept[rx] + keep
        zsum = zsum + keep * jnp.exp(sums[xi] - sums[0])
    for r in range(PEER_TOPK):
        l_ref[r] = kept[r]
    l_ref[PEER_TOPK] = 1.0 / zsum


def _peer_select_kernel(q_ref, key_ref, ea_ref, lr_ref, eb_ref, cr_ref, s_ref, rk_ref, a_ref, l_ref,
                        *, heads):
    nk, tt = s_ref.shape[1], s_ref.shape[2]
    nslab = nk // SUBLANES
    ambiguous = jnp.zeros((1, tt), F32)
    for hp in range(2 * heads):
        slot = (hp % 2) * heads + hp // 2
        s = lax.dot_general(key_ref[hp], q_ref[:, hp * LANES:(hp + 1) * LANES], (((1,), (1,)), ((), ())),
                            precision=lax.Precision.HIGHEST, preferred_element_type=F32)
        s_ref[slot] = s
        slabs = [s[v * SUBLANES:(v + 1) * SUBLANES, :] for v in range(nslab)]
        best = _top16_values(slabs)
        for r in range(PEER_TOPK):
            a_ref[r, slot:slot + 1, :] = best[r][0:1, :]
        at_least_last = jnp.zeros((SUBLANES, tt), F32)
        for v in range(nslab):
            at_least_last = at_least_last + jnp.where(slabs[v] >= best[PEER_TOPK - 1], 1.0, 0.0)
        bad = jnp.sum(at_least_last, axis=0, keepdims=True) != float(PEER_TOPK)
        for r in range(PEER_TOPK - 1):
            bad = jnp.logical_or(bad, best[r][0:1, :] == best[r + 1][0:1, :])
        ambiguous = jnp.maximum(ambiguous, jnp.where(bad, 1.0, 0.0))
    redo = jnp.max(ambiguous) > 0.0

    @pl.when(redo)
    def _():
        def one_slot(slot, carry):
            rk_ref[slot] = _top16_ranked(s_ref[slot], a_ref, slot)
            return carry
        lax.fori_loop(0, 2 * heads, one_slot, 0)

    _kept_per_row(a_ref, l_ref, heads)

    for h in range(heads):
        ea_ref[h] = jnp.exp(s_ref[h] - a_ref[0, h:h + 1, :]) * l_ref[PEER_TOPK, h:h + 1, :]
        eb_ref[h] = jnp.exp(s_ref[heads + h] - a_ref[0, heads + h:heads + h + 1, :]).astype(eb_ref.dtype)

    @pl.when(jnp.logical_not(redo))
    def _():
        for h in range(heads):
            s0, s1 = s_ref[h], s_ref[heads + h]
            lrow = jnp.zeros((nk, tt), F32)
            crank = jnp.full((nk, tt), float(PEER_TOPK), F32)
            for r in range(PEER_TOPK):
                lrow = jnp.where(s0 == a_ref[r, h:h + 1, :], l_ref[r, h:h + 1, :], lrow)
                crank = jnp.where(s1 == a_ref[r, heads + h:heads + h + 1, :], float(r), crank)
            lr_ref[h] = lrow
            cr_ref[h] = crank.astype(cr_ref.dtype)

    @pl.when(redo)
    def _():
        for h in range(heads):
            rk0 = rk_ref[h]
            lrow = jnp.zeros((nk, tt), F32)
            for r in range(PEER_TOPK):
                lrow = jnp.where(rk0 == float(r), l_ref[r, h:h + 1, :], lrow)
            lr_ref[h] = lrow
            cr_ref[h] = rk_ref[heads + h].astype(cr_ref.dtype)


def _peer_select(q, keys):
    n, dq = q.shape
    heads, _, nk, dk = keys.shape
    assert nk == PEER_TOPK * SUBLANES and dk == LANES
    tt = _pick(n, (256, 128))
    out = jax.ShapeDtypeStruct((heads, nk, n), F32)
    out_j = jax.ShapeDtypeStruct((heads, nk, n), BF16)
    ospec = pl.BlockSpec((heads, nk, tt), lambda t: (0, 0, t))
    return pl.pallas_call(
        functools.partial(_peer_select_kernel, heads=heads),
        grid=(n // tt,),
        in_specs=[
            pl.BlockSpec((tt, dq), lambda t: (t, 0)),
            pl.BlockSpec((2 * heads, nk, dk), lambda t: (0, 0, 0)),
        ],
        out_specs=[ospec, ospec, ospec, ospec],
        out_shape=[out, out, out_j, out_j],
        scratch_shapes=[
            pltpu.VMEM((2 * heads, nk, tt), F32),
            pltpu.VMEM((2 * heads, nk, tt), F32),
            pltpu.VMEM((PEER_TOPK, 2 * heads, tt), F32),
            pltpu.VMEM((PEER_TOPK + 1, heads, tt), F32),
        ],
        name="peer_select",
        compiler_params=_params("parallel"),
    )(q, keys.reshape(2 * heads, nk, dk))


def _peer_dense_kernel(h_ref, u_ref, vt_ref, ea_ref, lr_ref, eb_ref, cr_ref, o_ref, acc_ref, act_ref, coef_ref,
                       *, heads):
    ec, tt = act_ref.shape
    c = pl.program_id(1)

    @pl.when(c == 0)
    def _():
        acc_ref[...] = jnp.zeros_like(acc_ref)

    act_ref[...] = lax.dot_general(u_ref[...], h_ref[...], (((1,), (1,)), ((), ())),
                                   preferred_element_type=F32)

    def one_i(il, carry):
        rows = pl.ds(pl.multiple_of(il * LANES, LANES), LANES)
        wsum = jnp.zeros((LANES, tt), BF16)
        for h in range(heads):
            lrow = lr_ref[h, pl.ds(il, 1), :].astype(BF16)
            ea = ea_ref[h, pl.ds(il, 1), :].astype(BF16)
            wsum = wsum + jnp.where(cr_ref[h] < lrow, eb_ref[h] * ea, jnp.zeros((), BF16))
        g = jax.nn.gelu(act_ref[rows, :], approximate=True)
        coef_ref[rows, :] = (wsum.astype(F32) * g).astype(BF16)
        return carry

    lax.fori_loop(0, ec // LANES, one_i, 0)
    acc_ref[...] += jnp.dot(vt_ref[...], coef_ref[...], preferred_element_type=F32)

    @pl.when(c == pl.num_programs(1) - 1)
    def _():
        o_ref[...] = acc_ref[...].T


def _peer_dense(h_bf16, u_bf16, vt_bf16, ea, lr, eb, cr):
    n, d = h_bf16.shape
    e = u_bf16.shape[0]
    heads, nk, _ = ea.shape
    tt = _pick(n, (512, 256, 128))
    ec = _pick(e, (1024,))
    ipc = ec // LANES
    sel_i = pl.BlockSpec((heads, ipc, tt), lambda t, c: (0, c, t))
    sel_j = pl.BlockSpec((heads, nk, tt), lambda t, c: (0, 0, t))
    return pl.pallas_call(
        functools.partial(_peer_dense_kernel, heads=heads),
        grid=(n // tt, e // ec),
        in_specs=[
            pl.BlockSpec((tt, d), lambda t, c: (t, 0)),
            pl.BlockSpec((ec, d), lambda t, c: (c, 0)),
            pl.BlockSpec((d, ec), lambda t, c: (0, c)),
            sel_i, sel_i, sel_j, sel_j,
        ],
        out_specs=pl.BlockSpec((tt, d), lambda t, c: (t, 0)),
        out_shape=jax.ShapeDtypeStruct((n, d), F32),
        scratch_shapes=[pltpu.VMEM((d, tt), F32), pltpu.VMEM((ec, tt), F32), pltpu.VMEM((ec, tt), BF16)],
        name="peer_dense",
        compiler_params=_params("parallel", "arbitrary"),
    )(h_bf16, u_bf16, vt_bf16, ea, lr, eb, cr)


def _resid_kernel(x_ref, y_ref, g_ref, o_ref, *, group):
    for g in range(x_ref.shape[0] // group):
        rs = slice(g * group, (g + 1) * group)
        o_ref[rs, :] = x_ref[rs, :] + g_ref[g] * y_ref[rs, :]


def _gated_residual(x, y, gate, *, group):
    n, d = x.shape
    tr = _pick(n, (512, 256, 128, 64))
    row = pl.BlockSpec((tr, d), lambda i: (i, 0))
    return pl.pallas_call(
        functools.partial(_resid_kernel, group=group),
        grid=(n // tr,),
        in_specs=[row, row, pl.BlockSpec((tr // group, 1, d), lambda i: (i, 0, 0))],
        out_specs=row,
        out_shape=jax.ShapeDtypeStruct((n, d), F32),
        compiler_params=_params("parallel"),
    )(x, y, gate)


def _final_norm_kernel(x_ref, g_ref, o_ref):
    x = x_ref[...]
    o_ref[...] = x * lax.rsqrt(jnp.mean(x * x, axis=-1, keepdims=True) + EPS) * g_ref[...]


def _final_norm(x, gain):
    n, d = x.shape
    tr = _pick(n, (512, 256, 128, 64))
    row = pl.BlockSpec((tr, d), lambda i: (i, 0))
    return pl.pallas_call(
        _final_norm_kernel,
        grid=(n // tr,),
        in_specs=[row, pl.BlockSpec((1, d), lambda i: (0, 0))],
        out_specs=row,
        out_shape=jax.ShapeDtypeStruct((n, d), F32),
        compiler_params=_params("parallel"),
    )(x, gain.reshape(1, d))


def kernel(x_prompt, x_sample, cache_k, cache_v, state_conv, c_prompt, c_sample, w_ada, b_ada, norm_mix,
           w_in, w_conv, norm_att_out, norm_conv_out, w_out, norm_ffn, w_peer_q, peer_keys, peer_u, peer_v,
           norm_final):
    bp, tp, d = x_prompt.shape
    bs, ts, _ = x_sample.shape
    depth = w_in.shape[0]
    past = cache_k.shape[2]
    heads = cache_k.shape[3]
    w = heads * LANES
    assert cache_k.shape[4] == LANES and state_conv.shape[-1] == w and w_in.shape[2] == 6 * w
    assert tp % ts == 0 and ts % SUBLANES == 0
    group = ts
    n_p, n_s = bp * tp, bs * ts
    n = n_p + n_s
    scale = float(LANES) ** -0.5

    x = jnp.concatenate([x_prompt.reshape(n_p, d), x_sample.reshape(n_s, d)], axis=0)

    c_all = jnp.concatenate([c_prompt, c_sample], axis=0)
    r = c_all.shape[0]
    r_pad = -(-r // SUBLANES) * SUBLANES
    mod = _ada(jnp.pad(c_all, ((0, r_pad - r), (0, 0))), w_ada, b_ada)
    group_row = jnp.concatenate([jnp.repeat(jnp.arange(bp), tp // group), bp + jnp.arange(bs)])
    mod = jnp.take(mod, group_row, axis=1).reshape(depth, n // group, 6, 1, d)

    state =jnp.concatenate([jnp.zeros((depth, n_p // group, CONV_K - 1, w), F32), state_conv], axis=1)
    seq_end_group = jnp.concatenate([(jnp.arange(bp) + 1) * (tp // group) - 1, n_p // group + jnp.arange(bs)])

    kp, vp, cp, ks, vs, cs = [], [], [], [], [], []
    for l in range(depth):
        sh1, sc1, g1, sh2, sc2, g2 = (mod[l, :, k] for k in range(6))
        proj = _norm_mod_matmul(x, norm_mix[l], sc1, sh1, w_in[l].astype(BF16), group=group, emit_h=False)
        att_p = _attn_prompt(proj, batch=bp, seq=tp, heads=heads, scale=scale)
        att_s = _attn_sample_cached(proj, cache_k, cache_v, l, n_prompt=n_p, seq=ts, heads=heads, scale=scale)
        att = jnp.concatenate([att_p, att_s], axis=0)
        x, tails = _mix_out(att, proj, state[l], w_conv[l], norm_att_out[l], norm_conv_out[l],
                            w_out[l].astype(BF16), x, g1, group=group, n_prompt=n_p, seq=tp)
        q, h2 = _norm_mod_matmul(x, norm_ffn[l], sc2, sh2, w_peer_q[l].astype(BF16), group=group, emit_h=True)
        ea, lr, eb, cr = _peer_select(q, peer_keys[l])
        y = _peer_dense(h2, peer_u[l].astype(BF16), peer_v[l].astype(BF16).T, ea, lr, eb, cr)
        x = _gated_residual(x, y, g2, group=group)

        k_new, v_new = proj[:, w:2 * w], proj[:, 2 * w:3 * w]
        conv_new = jnp.take(tails, seq_end_group, axis=0)[:, SUBLANES - (CONV_K - 1):, :]
        kp.append(k_new[:n_p].reshape(bp, tp, heads, LANES))
        vp.append(v_new[:n_p].reshape(bp, tp, heads, LANES))
        ks.append(k_new[n_p:].reshape(bs, ts, heads, LANES))
        vs.append(v_new[n_p:].reshape(bs, ts, heads, LANES))
        cp.append(conv_new[:bp])
        cs.append(conv_new[bp:])

    y = _final_norm(x, norm_final)
    return (y[:n_p].reshape(bp, tp, d), y[n_p:].reshape(bs, ts, d), jnp.stack(kp), jnp.stack(vp),
            jnp.stack(cp), jnp.stack(ks), jnp.stack(vs), jnp.stack(cs))
```

```python
import functools

import jax
import jax.numpy as jnp
from jax import lax
from jax.experimental import pallas as pl
from jax.experimental.pallas import tpu as pltpu

F32 = jnp.float32
BF16 = jnp.bfloat16
EPS = 1e-6
LANES = 128
SUBLANES = 8
CONV_K = 3
PEER_TOPK = 16
VMEM_LIMIT_BYTES = 56 * 1024 * 1024
EXP_UNDERFLOW = -104.0


def _params(*sem):
    return pltpu.CompilerParams(dimension_semantics=sem, vmem_limit_bytes=VMEM_LIMIT_BYTES)


def _pick(n, cands):
    for c in cands:
        if n % c == 0:
            return c
    raise ValueError(f"no tile in {cands} divides {n}")


def _ada_kernel(c_ref, w_ref, b_ref, o_ref):
    c = c_ref[...]
    s = (c * jax.nn.sigmoid(c)).astype(BF16)
    o_ref[...] = jnp.dot(s, w_ref[...].astype(BF16), preferred_element_type=F32) + b_ref[...]


def _ada(c_all, w_ada, b_ada):
    depth, d, n6 = w_ada.shape
    r = c_all.shape[0]
    tn = _pick(n6, (1024, 512, 256, 128))
    return pl.pallas_call(
        _ada_kernel,
        grid=(depth, n6 // tn),
        in_specs=[
            pl.BlockSpec((r, d), lambda l, j: (0, 0)),
            pl.BlockSpec((None, d, tn), lambda l, j: (l, 0, j)),
            pl.BlockSpec((None, 1, tn), lambda l, j: (l, 0, j)),
        ],
        out_specs=pl.BlockSpec((None, r, tn), lambda l, j: (l, 0, j)),
        out_shape=jax.ShapeDtypeStruct((depth, r, n6), F32),
        name="ada",
        compiler_params=_params("parallel", "parallel"),
    )(c_all, w_ada, b_ada.reshape(depth, 1, n6))


def _nmm_kernel(x_ref, g_ref, sc_ref, sh_ref, w_ref, o_ref, *rest, group, emit_h):
    hs_ref = rest[-1]
    tr = x_ref.shape[0]

    @pl.when(pl.program_id(1) == 0)
    def _():
        x = x_ref[...]
        y = x * lax.rsqrt(jnp.mean(x * x, axis=-1, keepdims=True) + EPS) * g_ref[...]
        for g in range(tr // group):
            rows = slice(g * group, (g + 1) * group)
            hs_ref[rows, :] = (y[rows] * (1.0 + sc_ref[g]) + sh_ref[g]).astype(BF16)
        if emit_h:
            rest[0][...] = hs_ref[...]

    o_ref[...] = jnp.dot(hs_ref[...], w_ref[...], preferred_element_type=F32)


def _norm_mod_matmul(x, gain, sc, sh, w_bf16, *, group, emit_h):
    n, d = x.shape
    nout = w_bf16.shape[1]
    tr = _pick(n, (512, 256, 128, 64))
    tn = _pick(nout, (1024, 512, 256, 128))
    gpt = tr // group
    out_shape = [jax.ShapeDtypeStruct((n, nout), F32)]
    out_specs = [pl.BlockSpec((tr, tn), lambda i, j: (i, j))]
    if emit_h:
        out_shape.append(jax.ShapeDtypeStruct((n, d), BF16))
        out_specs.append(pl.BlockSpec((tr, d), lambda i, j: (i, 0)))
    res = pl.pallas_call(
        functools.partial(_nmm_kernel, group=group, emit_h=emit_h),
        grid=(n // tr, nout // tn),
        in_specs=[
            pl.BlockSpec((tr, d), lambda i, j: (i, 0)),
            pl.BlockSpec((1, d), lambda i, j: (0, 0)),
            pl.BlockSpec((gpt, 1, d), lambda i, j: (i, 0, 0)),
            pl.BlockSpec((gpt, 1, d), lambda i, j: (i, 0, 0)),
            pl.BlockSpec((d, tn), lambda i, j: (0, j)),
        ],
        out_specs=out_specs,
        out_shape=out_shape,
        scratch_shapes=[pltpu.VMEM((tr, d), BF16)],
        name="norm_mod_matmul",
        compiler_params=_params("parallel", "arbitrary"),
    )(x, gain.reshape(1, d), sc, sh, w_bf16)
    return res if emit_h else res[0]


def _strict_upper(n):
    return (lax.broadcasted_iota(jnp.int32, (n, n), 0) > lax.broadcasted_iota(jnp.int32, (n, n), 1)).astype(BF16)


def _sb_blocks(qs, ks, vs, tri, carries, accs, *, masked, scale):
    n = len(qs)
    tq = qs[0].shape[0]
    zs = [lax.dot_general(qs[h], ks[h], (((1,), (1,)), ((), ())), preferred_element_type=F32) * scale
          for h in range(n)]
    ls = [jnp.minimum(z, 0.0) - jnp.log1p(jnp.exp(-jnp.abs(z))) for z in zs]
    lr = [l - z for l, z in zip(ls, zs)]
    if masked:
        shape = zs[0].shape
        vis = lax.broadcasted_iota(jnp.int32, shape, 1) < lax.broadcasted_iota(jnp.int32, shape, 0)
        lr = [jnp.where(vis, x, 0.0) for x in lr]
    lr_all = jnp.concatenate(lr, axis=0)
    hi = lr_all.astype(BF16)
    r1 = lr_all - hi.astype(F32)
    mid = r1.astype(BF16)
    lo = (r1 - mid.astype(F32)).astype(BF16)
    suf = (jnp.dot(hi, tri, preferred_element_type=F32) + jnp.dot(mid, tri, preferred_element_type=F32)
           + jnp.dot(lo, tri, preferred_element_type=F32))
    ws = []
    for h in range(n):
        w = jnp.exp(ls[h] + suf[h * tq:(h + 1) * tq] + carries[h])
        if masked:
            w = jnp.where(vis, w, 0.0)
        ws.append(w.astype(BF16))
    accs = [accs[h] + jnp.dot(ws[h], vs[h], preferred_element_type=F32) for h in range(n)]
    carries = [carries[h] + jnp.sum(lr[h], axis=1, keepdims=True) for h in range(n)]
    return carries, accs


def _attn_prompt_kernel(q_ref, k_ref, v_ref, o_ref, *, heads, scale):
    tq = q_ref.shape[0]
    qi = pl.program_id(1)
    tri = _strict_upper(tq)
    cols = [slice(h * LANES, (h + 1) * LANES) for h in range(heads)]
    qs = [q_ref[:, cols[h]].astype(BF16) for h in range(heads)]

    def all_heads(kb, carries, accs, masked):
        rows = pl.ds(pl.multiple_of(kb * tq, tq), tq)
        return _sb_blocks(qs, [k_ref[rows, cols[h]].astype(BF16) for h in range(heads)],
                          [v_ref[rows, cols[h]].astype(BF16) for h in range(heads)], tri, carries, accs,
                          masked=masked, scale=scale)

    carries, accs = all_heads(qi, [jnp.zeros((tq, 1), F32)] * heads, [jnp.zeros((tq, LANES), F32)] * heads, True)

    def cond(st):
        kb, carries, _ = st
        live = functools.reduce(jnp.maximum, carries)
        return jnp.logical_and(kb >= 0, jnp.max(live) > EXP_UNDERFLOW)

    def body(st):
        kb, carries, accs = st
        carries, accs = all_heads(kb, carries, accs, False)
        return kb - 1, carries, accs

    _, _, accs = lax.while_loop(cond, body, (qi - 1, carries, accs))
    for h in range(heads):
        o_ref[:, cols[h]] = accs[h]


def _attn_prompt(proj, *, batch, seq, heads, scale):
    w = heads * LANES
    tq = _pick(seq, (128, 64))
    nq = seq // tq
    return pl.pallas_call(
        functools.partial(_attn_prompt_kernel, heads=heads, scale=scale),
        grid=(batch, nq),
        in_specs=[
            pl.BlockSpec((tq, w), lambda b, i: (b * nq + i, 0)),
            pl.BlockSpec((seq, w), lambda b, i: (b, 1)),
            pl.BlockSpec((seq, w), lambda b, i: (b, 2)),
        ],
        out_specs=pl.BlockSpec((tq, w), lambda b, i: (b * nq + i, 0)),
        out_shape=jax.ShapeDtypeStruct((batch * seq, w), F32),
        name="attn_prompt",
        compiler_params=_params("parallel", "parallel"),
    )(proj, proj, proj)


def _attn_sample_kernel(q_ref, kn_ref, vn_ref, kc_ref, vc_ref, o_ref, cmax_ref, carry_ref, acc_ref,
                        *, heads, tk, scale):
    tq = q_ref.shape[0]
    tkp = kc_ref.shape[0]
    j = pl.program_id(1)

    def hs(h):
        return slice(h * LANES, (h + 1) * LANES)

    def qs():
        return [q_ref[:, hs(h)].astype(BF16) for h in range(heads)]

    def put(carries, accs):
        for h in range(heads):
            carry_ref[h] = jnp.broadcast_to(carries[h], (tq, LANES))
            acc_ref[:, hs(h)] = accs[h]

    @pl.when(j == 0)
    def _():
        put(*_sb_blocks(qs(), [kn_ref[:, hs(h)].astype(BF16) for h in range(heads)],
                        [vn_ref[:, hs(h)].astype(BF16) for h in range(heads)], _strict_upper(tq),
                        [jnp.zeros((tq, 1), F32)] * heads, [jnp.zeros((tq, LANES), F32)] * heads,
                        masked=True, scale=scale))

    @pl.when(j > 0)
    def _():
        tri = _strict_upper(tk)
        for sb in reversed(range(tkp // tk)):
            live = functools.reduce(jnp.maximum, [carry_ref[h] for h in range(heads)])

            @pl.when(jnp.max(live) > EXP_UNDERFLOW)
            def _():
                rows = slice(sb * tk, (sb + 1) * tk)
                put(*_sb_blocks(qs(), [kc_ref[rows, hs(h)].astype(BF16) for h in range(heads)],
                                [vc_ref[rows, hs(h)].astype(BF16) for h in range(heads)], tri,
                                [carry_ref[h][:, 0:1] for h in range(heads)],
                                [acc_ref[:, hs(h)] for h in range(heads)], masked=False, scale=scale))

    @pl.when(j == pl.num_programs(1) - 1)
    def _():
        o_ref[...] = acc_ref[...]
        for h in range(heads):
            cmax_ref[h:h + 1, :] = jnp.max(carry_ref[h], axis=0, keepdims=True)


SAMPLE_CACHE_BLOCK = 512


def _attn_sample(proj, cache_k, cache_v, *, n_prompt, seq, heads, scale):
    batch, past, w = cache_k.shape
    tkp = _pick(past, (SAMPLE_CACHE_BLOCK, 256, 128))
    tk = min(tkp, 256)
    n_past = past // tkp
    row0 = n_prompt // seq
    cache_spec = pl.BlockSpec((None, tkp, w), lambda b, j: (b, n_past - jnp.maximum(j, 1), 0))
    return pl.pallas_call(
        functools.partial(_attn_sample_kernel, heads=heads, tk=tk, scale=scale),
        grid=(batch, 1 + n_past),
        in_specs=[
            pl.BlockSpec((seq, w), lambda b, j: (row0 + b, 0)),
            pl.BlockSpec((seq, w), lambda b, j: (row0 + b, 1)),
            pl.BlockSpec((seq, w), lambda b, j: (row0 + b, 2)),
            cache_spec,
            cache_spec,
        ],
        out_specs=[
            pl.BlockSpec((seq, w), lambda b, j: (b, 0)),
            pl.BlockSpec((None, heads, LANES), lambda b, j: (b, 0, 0)),
        ],
        out_shape=[
            jax.ShapeDtypeStruct((batch * seq, w), F32),
            jax.ShapeDtypeStruct((batch, heads, LANES), F32),
        ],
        scratch_shapes=[pltpu.VMEM((heads, seq, LANES), F32), pltpu.VMEM((seq, w), F32)],
        name="attn_sample",
        compiler_params=_params("parallel", "arbitrary"),
    )(proj, proj, proj, cache_k, cache_v)


def _attn_sample_cached(proj, cache_k, cache_v, layer, *, n_prompt, seq, heads, scale):
    batch, past = cache_k.shape[1:3]
    w = heads * LANES
    kw = dict(n_prompt=n_prompt, seq=seq, heads=heads, scale=scale)
    recent = min(past, SAMPLE_CACHE_BLOCK)
    att, cmax = _attn_sample(proj, cache_k[layer, :, past - recent:].reshape(batch, recent, w),
                             cache_v[layer, :, past - recent:].reshape(batch, recent, w), **kw)
    if recent == past:
        return att
    return lax.cond(
        jnp.max(cmax) > EXP_UNDERFLOW,
        lambda ck, cv: _attn_sample(proj, ck[layer].reshape(batch, past, w), cv[layer].reshape(batch, past, w),
                                    **kw)[0],
        lambda ck, cv: att,
        cache_k, cache_v)


def _mix_kernel(att_ref, bg_ref, cg_ref, xc_ref, cgh_ref, xch_ref, st_ref, wc_ref, na_ref, nc_ref,
                wo_ref, x_ref, g1_ref, o_ref, tail_ref, um1_ref, um2_ref, mix_ref,
                *, group, prompt_groups, groups_per_seq):
    tr, w = att_ref.shape
    gpt = tr // group
    i = pl.program_id(0)

    u = cg_ref[...] * xc_ref[...]
    uh = cgh_ref[...] * xch_ref[...]
    rows = lax.broadcasted_iota(jnp.int32, (tr, w), 0)
    h7 = uh[SUBLANES - 1:SUBLANES, :]
    h6 = uh[SUBLANES - 2:SUBLANES - 1, :]
    um1_ref[...] = jnp.where(rows == 0, h7, pltpu.roll(u, 1, axis=0))
    um2_ref[...] = jnp.where(rows == 0, h6, jnp.where(rows == 1, h7, pltpu.roll(u, 2, axis=0)))
    r8 = lax.broadcasted_iota(jnp.int32, (SUBLANES, w), 0)
    for g in range(gpt):
        gg = i * gpt + g
        starts_sequence = jnp.logical_or(gg >= prompt_groups, gg % groups_per_seq == 0)
        head = slice(g * group, g * group + SUBLANES)

        @pl.when(starts_sequence)
        def _():
            st0 = st_ref[g, 0:1, :]
            st1 = st_ref[g, 1:2, :]
            um1_ref[head, :] = jnp.where(r8 == 0, st1, um1_ref[head, :])
            um2_ref[head, :] = jnp.where(r8 == 0, st0, jnp.where(r8 == 1, st1, um2_ref[head, :]))

        tail_ref[g] = u[(g + 1) * group - SUBLANES:(g + 1) * group, :]

    conv = wc_ref[0:1, :] * um2_ref[...] + wc_ref[1:2, :] * um1_ref[...] + wc_ref[2:3, :] * u
    cv = bg_ref[...] * conv
    att = att_ref[...]
    for k in range(w // LANES):
        cs = slice(k * LANES, (k + 1) * LANES)
        a = att[:, cs]
        mix_ref[:, cs] = (a * lax.rsqrt(jnp.mean(a * a, axis=-1, keepdims=True) + EPS)
                          * na_ref[:, cs]).astype(BF16)
        c = cv[:, cs]
        mix_ref[:, w + k * LANES:w + (k + 1) * LANES] = (
            c * lax.rsqrt(jnp.mean(c * c, axis=-1, keepdims=True) + EPS) * nc_ref[:, cs]).astype(BF16)
    y = jnp.dot(mix_ref[...], wo_ref[...], preferred_element_type=F32)
    for g in range(gpt):
        rs = slice(g * group, (g + 1) * group)
        o_ref[rs, :] = x_ref[rs, :] + g1_ref[g] * y[rs]


def _mix_out(att, proj, state, w_conv, norm_att, norm_conv, w_out_bf16, x, g1, *, group, n_prompt, seq):
    n, d = x.shape
    w = att.shape[1]
    tr = _pick(n, (256, 128, 64))
    gpt = tr // group
    hb = tr // SUBLANES
    kern = functools.partial(_mix_kernel, group=group, prompt_groups=n_prompt // group,
                             groups_per_seq=seq // group)
    halo = lambda col: pl.BlockSpec((SUBLANES, w), lambda i: (jnp.maximum(i * hb - 1, 0), col))
    return pl.pallas_call(
        kern,
        grid=(n // tr,),
        in_specs=[
            pl.BlockSpec((tr, w), lambda i: (i, 0)),
            pl.BlockSpec((tr, w), lambda i: (i, 3)),
            pl.BlockSpec((tr, w), lambda i: (i, 4)),
            pl.BlockSpec((tr, w), lambda i: (i, 5)),
            halo(4),
            halo(5),
            pl.BlockSpec((gpt, CONV_K - 1, w), lambda i: (i, 0, 0)),
            pl.BlockSpec((CONV_K, w), lambda i: (0, 0)),
            pl.BlockSpec((1, w), lambda i: (0, 0)),
            pl.BlockSpec((1, w), lambda i: (0, 0)),
            pl.BlockSpec((2 * w, d), lambda i: (0, 0)),
            pl.BlockSpec((tr, d), lambda i: (i, 0)),
            pl.BlockSpec((gpt, 1, d), lambda i: (i, 0, 0)),
        ],
        out_specs=[
            pl.BlockSpec((tr, d), lambda i: (i, 0)),
            pl.BlockSpec((gpt, SUBLANES, w), lambda i: (i, 0, 0)),
        ],
        out_shape=[
            jax.ShapeDtypeStruct((n, d), F32),
            jax.ShapeDtypeStruct((n // group, SUBLANES, w), F32),
        ],
        scratch_shapes=[pltpu.VMEM((tr, w), F32), pltpu.VMEM((tr, w), F32), pltpu.VMEM((tr, 2 * w), BF16)],
        name="mix_out",
        compiler_params=_params("parallel"),
    )(att, proj, proj, proj, proj, proj, state, w_conv, norm_att.reshape(1, w), norm_conv.reshape(1, w),
      w_out_bf16, x, g1)


def _staircase():
    return [(r, c) for r in range(PEER_TOPK) for c in range(PEER_TOPK) if (r + 1) * (c + 1) <= PEER_TOPK]


def _sort16_network():
    pairs, n, p = [], 16, 1
    while p < n:
        k = p
        while k >= 1:
            for j in range(k % p, n - k, 2 * k):
                for i in range(min(k, n - j - k)):
                    if (i + j) // (2 * p) == (i + j + k) // (2 * p):
                        pairs.append((i + j, i + j + k))
            k //= 2
        p *= 2
    return pairs


def _top16_values(slabs):
    x = list(slabs)
    for i, j in _sort16_network():
        x[i], x[j] = jnp.maximum(x[i], x[j]), jnp.minimum(x[i], x[j])
    for shift in (4, 2, 1):
        z = [jnp.maximum(x[r], pltpu.roll(x[PEER_TOPK - 1 - r], shift, axis=0)) for r in range(PEER_TOPK)]
        for dist in (8, 4, 2, 1):
            for i in range(PEER_TOPK):
                if i & dist == 0:
                    z[i], z[i + dist] = jnp.maximum(z[i], z[i + dist]), jnp.minimum(z[i], z[i + dist])
        x = z
    return x


def _top16_ranked(s, a_ref, slot):
    nk, tt = s.shape
    kidx = lax.broadcasted_iota(jnp.int32, (nk, tt), 0).astype(F32)
    work = s
    rank = jnp.full((nk, tt), float(PEER_TOPK), F32)
    for r in range(PEER_TOPK):
        m = jnp.max(work, axis=0, keepdims=True)
        first = jnp.min(jnp.where(work == m, kidx, float(nk)), axis=0, keepdims=True)
        hit = kidx == first
        rank = jnp.where(hit, float(r), rank)
        work = jnp.where(hit, -jnp.inf, work)
        a_ref[r, pl.ds(slot, 1), :] = m
    return rank


def _kept_per_row(a_ref, l_ref, heads):
    cands = _staircase()
    a0 = [a_ref[r, 0:heads, :] for r in range(PEER_TOPK)]
    a1 = [a_ref[c, heads:2 * heads, :] for c in range(PEER_TOPK)]
    sums = [a0[r] + a1[c] for r, c in cands]
    ahead = [0.0] * len(cands)
    for xi, (rx, cx) in enumerate(cands):
        for yi in range(xi):
            ry, cy = cands[yi]
            if ry <= rx and cy <= cx:
                ahead[xi] = ahead[xi] + 1.0
            else:
                y_first = sums[yi] >= sums[xi]
                ahead[xi] = ahead[xi] + jnp.where(y_first, 1.0, 0.0)
                ahead[yi] = ahead[yi] + jnp.where(y_first, 0.0, 1.0)
    zsum = jnp.zeros_like(sums[0])
    kept = [jnp.zeros_like(sums[0]) for _ in range(PEER_TOPK)]
    for xi, (rx, cx) in enumerate(cands):
        keep = jnp.where(ahead[xi] < float(PEER_TOPK), 1.0, 0.0)
        kept[rx] = kept[rx] + keep
        zsum = zsum + keep * jnp.exp(sums[xi] - sums[0])
    for r in range(PEER_TOPK):
        l_ref[r] = kept[r]
    l_ref[PEER_TOPK] = 1.0 / zsum


def _peer_select_kernel(q_ref, key_ref, ea_ref, lr_ref, eb_ref, cr_ref, s_ref, rk_ref, a_ref, l_ref,
                        *, heads):
    nk, tt = s_ref.shape[1], s_ref.shape[2]
    nslab = nk // SUBLANES
    ambiguous = jnp.zeros((1, tt), F32)
    for hp in range(2 * heads):
        slot = (hp % 2) * heads + hp // 2
        s = lax.dot_general(key_ref[hp], q_ref[:, hp * LANES:(hp + 1) * LANES], (((1,), (1,)), ((), ())),
                            precision=lax.Precision.HIGHEST, preferred_element_type=F32)
        s_ref[slot] = s
        slabs = [s[v * SUBLANES:(v + 1) * SUBLANES, :] for v in range(nslab)]
        best = _top16_values(slabs)
        for r in range(PEER_TOPK):
            a_ref[r, slot:slot + 1, :] = best[r][0:1, :]
        at_least_last = jnp.zeros((SUBLANES, tt), F32)
        for v in range(nslab):
            at_least_last = at_least_last + jnp.where(slabs[v] >= best[PEER_TOPK - 1], 1.0, 0.0)
        bad = jnp.sum(at_least_last, axis=0, keepdims=True) != float(PEER_TOPK)
        for r in range(PEER_TOPK - 1):
            bad = jnp.logical_or(bad, best[r][0:1, :] == best[r + 1][0:1, :])
        ambiguous = jnp.maximum(ambiguous, jnp.where(bad, 1.0, 0.0))
    redo = jnp.max(ambiguous) > 0.0

    @pl.when(redo)
    def _():
        def one_slot(slot, carry):
            rk_ref[slot] = _top16_ranked(s_ref[slot], a_ref, slot)
            return carry
        lax.fori_loop(0, 2 * heads, one_slot, 0)

    _kept_per_row(a_ref, l_ref, heads)

    for h in range(heads):
        ea_ref[h] = jnp.exp(s_ref[h] - a_ref[0, h:h + 1, :]) * l_ref[PEER_TOPK, h:h + 1, :]
        eb_ref[h] = jnp.exp(s_ref[heads + h] - a_ref[0, heads + h:heads + h + 1, :]).astype(eb_ref.dtype)

    @pl.when(jnp.logical_not(redo))
    def _():
        for h in range(heads):
            s0, s1 = s_ref[h], s_ref[heads + h]
            lrow = jnp.zeros((nk, tt), F32)
            crank = jnp.full((nk, tt), float(PEER_TOPK), F32)
            for r in range(PEER_TOPK):
                lrow = jnp.where(s0 == a_ref[r, h:h + 1, :], l_ref[r, h:h + 1, :], lrow)
                crank = jnp.where(s1 == a_ref[r, heads + h:heads + h + 1, :], float(r), crank)
            lr_ref[h] = lrow
            cr_ref[h] = crank.astype(cr_ref.dtype)

    @pl.when(redo)
    def _():
        for h in range(heads):
            rk0 = rk_ref[h]
            lrow = jnp.zeros((nk, tt), F32)
            for r in range(PEER_TOPK):
                lrow = jnp.where(rk0 == float(r), l_ref[r, h:h + 1, :], lrow)
            lr_ref[h] = lrow
            cr_ref[h] = rk_ref[heads + h].astype(cr_ref.dtype)


def _peer_select(q, keys):
    n, dq = q.shape
    heads, _, nk, dk = keys.shape
    assert nk == PEER_TOPK * SUBLANES and dk == LANES
    tt = _pick(n, (256, 128))
    out = jax.ShapeDtypeStruct((heads, nk, n), F32)
    out_j = jax.ShapeDtypeStruct((heads, nk, n), BF16)
    ospec = pl.BlockSpec((heads, nk, tt), lambda t: (0, 0, t))
    return pl.pallas_call(
        functools.partial(_peer_select_kernel, heads=heads),
        grid=(n // tt,),
        in_specs=[
            pl.BlockSpec((tt, dq), lambda t: (t, 0)),
            pl.BlockSpec((2 * heads, nk, dk), lambda t: (0, 0, 0)),
        ],
        out_specs=[ospec, ospec, ospec, ospec],
        out_shape=[out, out, out_j, out_j],
        scratch_shapes=[
            pltpu.VMEM((2 * heads, nk, tt), F32),
            pltpu.VMEM((2 * heads, nk, tt), F32),
            pltpu.VMEM((PEER_TOPK, 2 * heads, tt), F32),
            pltpu.VMEM((PEER_TOPK + 1, heads, tt), F32),
        ],
        name="peer_select",
        compiler_params=_params("parallel"),
    )(q, keys.reshape(2 * heads, nk, dk))


PEER_CHUNK = 1024


def _peer_dense_kernel(h_ref, u_ref, vt_ref, vt_last_ref, ea_ref, lr_ref, eb_ref, cr_ref, o_ref,
                       acc_ref, act_ref, coef_lo_ref, coef_hi_ref, *, heads):
    tt = act_ref.shape[1]
    c = pl.program_id(1)
    one_trip = jnp.minimum(c, 0) + 1
    iph = PEER_CHUNK // 2 // LANES

    @pl.when(jnp.logical_and(pl.program_id(0) == 0, c == 0))
    def _():
        coef_hi_ref[...] = jnp.zeros_like(coef_hi_ref)

    @pl.when(c == 0)
    def _():
        acc_ref[...] = jnp.zeros_like(acc_ref)

    act_ref[...] = lax.dot_general(u_ref[...], h_ref[...], (((1,), (1,)), ((), ())),
                                   preferred_element_type=F32)

    def build(half, dst):
        tb = min(tt, 2 * LANES)
        for s in range(iph):
            il = half * iph + s
            rows = slice(il * LANES, (il + 1) * LANES)
            for t0 in range(0, tt, tb):
                cols = slice(t0, t0 + tb)
                wsum = jnp.zeros((LANES, tb), BF16)
                for h in range(heads):
                    lrow = lr_ref[h, il:il + 1, cols].astype(BF16)
                    ea = ea_ref[h, il:il + 1, cols].astype(BF16)
                    wsum = wsum + jnp.where(cr_ref[h, :, cols] < lrow, eb_ref[h, :, cols] * ea,
                                            jnp.zeros((), BF16))
                g = jax.nn.gelu(act_ref[rows, cols], approximate=True)
                dst[s * LANES:(s + 1) * LANES, cols] = (wsum.astype(F32) * g).astype(BF16)

    def region(half, src, dst):
        def body(_, carry):
            build(half, dst)
            acc_ref[...] += jnp.dot(vt_ref[half], src[...], preferred_element_type=F32)
            return carry
        lax.fori_loop(0, one_trip, body, 0)

    region(0, coef_hi_ref, coef_lo_ref)
    region(1, coef_lo_ref, coef_hi_ref)

    @pl.when(c == pl.num_programs(1) - 1)
    def _():
        acc_ref[...] += jnp.dot(vt_last_ref[0], coef_hi_ref[...], preferred_element_type=F32)
        o_ref[...] = acc_ref[...].T


def _peer_dense(h_bf16, u_bf16, v_bf16, ea, lr, eb, cr):
    n, d = h_bf16.shape
    e = u_bf16.shape[0]
    heads, nk, _ = ea.shape
    assert e % PEER_CHUNK == 0
    tt = _pick(n, (512, 256, 128))
    half = PEER_CHUNK // 2
    nhalves = e // half
    vt = v_bf16.reshape(nhalves, half, d).transpose(0, 2, 1)
    vt = jnp.concatenate([jnp.zeros((1, d, half), BF16), vt], axis=0)
    sel_i = pl.BlockSpec((heads, PEER_CHUNK // LANES, tt), lambda t, c: (0, c, t))
    sel_j = pl.BlockSpec((heads, nk, tt), lambda t, c: (0, 0, t))
    return pl.pallas_call(
        functools.partial(_peer_dense_kernel, heads=heads),
        grid=(n // tt, e // PEER_CHUNK),
        in_specs=[
            pl.BlockSpec((tt, d), lambda t, c: (t, 0)),
            pl.BlockSpec((PEER_CHUNK, d), lambda t, c: (c, 0)),
            pl.BlockSpec((2, d, half), lambda t, c: (c, 0, 0)),
            pl.BlockSpec((1, d, half), lambda t, c: (nhalves, 0, 0)),
            sel_i, sel_i, sel_j, sel_j,
        ],
        out_specs=pl.BlockSpec((tt, d), lambda t, c: (t, 0)),
        out_shape=jax.ShapeDtypeStruct((n, d), F32),
        scratch_shapes=[pltpu.VMEM((d, tt), F32), pltpu.VMEM((PEER_CHUNK, tt), F32),
                        pltpu.VMEM((half, tt), BF16), pltpu.VMEM((half, tt), BF16)],
        name="peer_dense",
        compiler_params=_params("arbitrary", "arbitrary"),
    )(h_bf16, u_bf16, vt, vt, ea, lr, eb, cr)


def _resid_kernel(x_ref, y_ref, g_ref, o_ref, *, group):
    for g in range(x_ref.shape[0] // group):
        rs = slice(g * group, (g + 1) * group)
        o_ref[rs, :] = x_ref[rs, :] + g_ref[g] * y_ref[rs, :]


def _gated_residual(x, y, gate, *, group):
    n, d = x.shape
    tr = _pick(n, (512, 256, 128, 64))
    row = pl.BlockSpec((tr, d), lambda i: (i, 0))
    return pl.pallas_call(
        functools.partial(_resid_kernel, group=group),
        grid=(n // tr,),
        in_specs=[row, row, pl.BlockSpec((tr // group, 1, d), lambda i: (i, 0, 0))],
        out_specs=row,
        out_shape=jax.ShapeDtypeStruct((n, d), F32),
        compiler_params=_params("parallel"),
    )(x, y, gate)


def _final_norm_kernel(x_ref, g_ref, o_ref):
    x = x_ref[...]
    o_ref[...] = x * lax.rsqrt(jnp.mean(x * x, axis=-1, keepdims=True) + EPS) * g_ref[...]


def _final_norm(x, gain):
    n, d = x.shape
    tr = _pick(n, (512, 256, 128, 64))
    row = pl.BlockSpec((tr, d), lambda i: (i, 0))
    return pl.pallas_call(
        _final_norm_kernel,
        grid=(n // tr,),
        in_specs=[row, pl.BlockSpec((1, d), lambda i: (0, 0))],
        out_specs=row,
        out_shape=jax.ShapeDtypeStruct((n, d), F32),
        compiler_params=_params("parallel"),
    )(x, gain.reshape(1, d))


def kernel(x_prompt, x_sample, cache_k, cache_v, state_conv, c_prompt, c_sample, w_ada, b_ada, norm_mix,
           w_in, w_conv, norm_att_out, norm_conv_out, w_out, norm_ffn, w_peer_q, peer_keys, peer_u, peer_v,
           norm_final):
    bp, tp, d = x_prompt.shape
    bs, ts, _ = x_sample.shape
    depth = w_in.shape[0]
    past = cache_k.shape[2]
    heads = cache_k.shape[3]
    w = heads * LANES
    assert cache_k.shape[4] == LANES and state_conv.shape[-1] == w and w_in.shape[2] == 6 * w
    assert tp % ts == 0 and ts % SUBLANES == 0
    group = ts
    n_p, n_s = bp * tp, bs * ts
    n = n_p + n_s
    scale = float(LANES) ** -0.5

    x = jnp.concatenate([x_prompt.reshape(n_p, d), x_sample.reshape(n_s, d)], axis=0)

    c_all = jnp.concatenate([c_prompt, c_sample], axis=0)
    r = c_all.shape[0]
    r_pad = -(-r // SUBLANES) * SUBLANES
    mod = _ada(jnp.pad(c_all, ((0, r_pad - r), (0, 0))), w_ada, b_ada)
    group_row = jnp.concatenate([jnp.repeat(jnp.arange(bp), tp // group), bp + jnp.arange(bs)])
    mod = jnp.take(mod, group_row, axis=1).reshape(depth, n // group, 6, 1, d)

    state =jnp.concatenate([jnp.zeros((depth, n_p // group, CONV_K - 1, w), F32), state_conv], axis=1)
    seq_end_group = jnp.concatenate([(jnp.arange(bp) + 1) * (tp // group) - 1, n_p // group + jnp.arange(bs)])

    kp, vp, cp, ks, vs, cs = [], [], [], [], [], []
    for l in range(depth):
        sh1, sc1, g1, sh2, sc2, g2 = (mod[l, :, k] for k in range(6))
        proj = _norm_mod_matmul(x, norm_mix[l], sc1, sh1, w_in[l].astype(BF16), group=group, emit_h=False)
        att_p = _attn_prompt(proj, batch=bp, seq=tp, heads=heads, scale=scale)
        att_s = _attn_sample_cached(proj, cache_k, cache_v, l, n_prompt=n_p, seq=ts, heads=heads, scale=scale)
        att = jnp.concatenate([att_p, att_s], axis=0)
        x, tails = _mix_out(att, proj, state[l], w_conv[l], norm_att_out[l], norm_conv_out[l],
                            w_out[l].astype(BF16), x, g1, group=group, n_prompt=n_p, seq=tp)
        q, h2 = _norm_mod_matmul(x, norm_ffn[l], sc2, sh2, w_peer_q[l].astype(BF16), group=group, emit_h=True)
        ea, lr, eb, cr = _peer_select(q, peer_keys[l])
        y = _peer_dense(h2, peer_u[l].astype(BF16), peer_v[l].astype(BF16), ea, lr, eb, cr)
        x = _gated_residual(x, y, g2, group=group)

        k_new, v_new = proj[:, w:2 * w], proj[:, 2 * w:3 * w]
        conv_new = jnp.take(tails, seq_end_group, axis=0)[:, SUBLANES - (CONV_K - 1):, :]
        kp.append(k_new[:n_p].reshape(bp, tp, heads, LANES))
        vp.append(v_new[:n_p].reshape(bp, tp, heads, LANES))
        ks.append(k_new[n_p:].reshape(bs, ts, heads, LANES))
        vs.append(v_new[n_p:].reshape(bs, ts, heads, LANES))
        cp.append(conv_new[:bp])
        cs.append(conv_new[bp:])

    y = _final_norm(x, norm_final)
    return (y[:n_p].reshape(bp, tp, d), y[n_p:].reshape(bs, ts, d), jnp.stack(kp), jnp.stack(vp),
            jnp.stack(cp), jnp.stack(ks), jnp.stack(vs), jnp.stack(cs))
```

```python
import functools

import jax
import jax.numpy as jnp
from jax import lax
from jax.experimental import pallas as pl
from jax.experimental.pallas import tpu as pltpu

F32 = jnp.float32
BF16 = jnp.bfloat16
EPS = 1e-6
LANES = 128
SUBLANES = 8
CONV_K = 3
PEER_TOPK = 16
VMEM_LIMIT_BYTES = 56 * 1024 * 1024
EXP_UNDERFLOW = -104.0


def _params(*sem):
    return pltpu.CompilerParams(dimension_semantics=sem, vmem_limit_bytes=VMEM_LIMIT_BYTES)


def _pick(n, cands):
    for c in cands:
        if n % c == 0:
            return c
    raise ValueError(f"no tile in {cands} divides {n}")


def _ada_kernel(c_ref, w_ref, b_ref, o_ref):
    c = c_ref[...]
    s = (c * jax.nn.sigmoid(c)).astype(BF16)
    o_ref[...] = jnp.dot(s, w_ref[...].astype(BF16), preferred_element_type=F32) + b_ref[...]


def _ada(c_all, w_ada, b_ada):
    depth, d, n6 = w_ada.shape
    r = c_all.shape[0]
    tn = _pick(n6, (1024, 512, 256, 128))
    return pl.pallas_call(
        _ada_kernel,
        grid=(depth, n6 // tn),
        in_specs=[
            pl.BlockSpec((r, d), lambda l, j: (0, 0)),
            pl.BlockSpec((None, d, tn), lambda l, j: (l, 0, j)),
            pl.BlockSpec((None, 1, tn), lambda l, j: (l, 0, j)),
        ],
        out_specs=pl.BlockSpec((None, r, tn), lambda l, j: (l, 0, j)),
        out_shape=jax.ShapeDtypeStruct((depth, r, n6), F32),
        name="ada",
        compiler_params=_params("parallel", "parallel"),
    )(c_all, w_ada, b_ada.reshape(depth, 1, n6))


def _nmm_kernel(x_ref, g_ref, sc_ref, sh_ref, w_ref, o_ref, *rest, group, emit_h):
    hs_ref = rest[-1]
    tr = x_ref.shape[0]

    @pl.when(pl.program_id(1) == 0)
    def _():
        x = x_ref[...]
        y = x * lax.rsqrt(jnp.mean(x * x, axis=-1, keepdims=True) + EPS) * g_ref[...]
        for g in range(tr // group):
            rows = slice(g * group, (g + 1) * group)
            hs_ref[rows, :] = (y[rows] * (1.0 + sc_ref[g]) + sh_ref[g]).astype(BF16)
        if emit_h:
            rest[0][...] = hs_ref[...]

    o_ref[...] = jnp.dot(hs_ref[...], w_ref[...], preferred_element_type=F32)


def _norm_mod_matmul(x, gain, sc, sh, w_bf16, *, group, emit_h):
    n, d = x.shape
    nout = w_bf16.shape[1]
    tr = _pick(n, (512, 256, 128, 64))
    tn = _pick(nout, (2048, 1024, 512, 256, 128))
    gpt = tr // group
    out_shape = [jax.ShapeDtypeStruct((n, nout), F32)]
    out_specs = [pl.BlockSpec((tr, tn), lambda i, j: (i, j))]
    if emit_h:
        out_shape.append(jax.ShapeDtypeStruct((n, d), BF16))
        out_specs.append(pl.BlockSpec((tr, d), lambda i, j: (i, 0)))
    res = pl.pallas_call(
        functools.partial(_nmm_kernel, group=group, emit_h=emit_h),
        grid=(n // tr, nout // tn),
        in_specs=[
            pl.BlockSpec((tr, d), lambda i, j: (i, 0)),
            pl.BlockSpec((1, d), lambda i, j: (0, 0)),
            pl.BlockSpec((gpt, 1, d), lambda i, j: (i, 0, 0)),
            pl.BlockSpec((gpt, 1, d), lambda i, j: (i, 0, 0)),
            pl.BlockSpec((d, tn), lambda i, j: (0, j)),
        ],
        out_specs=out_specs,
        out_shape=out_shape,
        scratch_shapes=[pltpu.VMEM((tr, d), BF16)],
        name="norm_mod_matmul",
        compiler_params=_params("parallel", "arbitrary"),
    )(x, gain.reshape(1, d), sc, sh, w_bf16)
    return res if emit_h else res[0]


def _strict_upper(n):
    return (lax.broadcasted_iota(jnp.int32, (n, n), 0) > lax.broadcasted_iota(jnp.int32, (n, n), 1)).astype(BF16)


def _sb_blocks(qs, ks, vs, tri, carries, accs, *, masked, scale):
    n = len(qs)
    tq = qs[0].shape[0]
    zs = [lax.dot_general(qs[h], ks[h], (((1,), (1,)), ((), ())), preferred_element_type=F32) * scale
          for h in range(n)]
    ls = [jnp.minimum(z, 0.0) - jnp.log1p(jnp.exp(-jnp.abs(z))) for z in zs]
    lr = [l - z for l, z in zip(ls, zs)]
    if masked:
        shape = zs[0].shape
        vis = lax.broadcasted_iota(jnp.int32, shape, 1) < lax.broadcasted_iota(jnp.int32, shape, 0)
        lr = [jnp.where(vis, x, 0.0) for x in lr]
    lr_all = jnp.concatenate(lr, axis=0)
    hi = lr_all.astype(BF16)
    r1 = lr_all - hi.astype(F32)
    mid = r1.astype(BF16)
    lo = (r1 - mid.astype(F32)).astype(BF16)
    suf = (jnp.dot(hi, tri, preferred_element_type=F32) + jnp.dot(mid, tri, preferred_element_type=F32)
           + jnp.dot(lo, tri, preferred_element_type=F32))
    ws = []
    for h in range(n):
        w = jnp.exp(ls[h] + suf[h * tq:(h + 1) * tq] + carries[h])
        if masked:
            w = jnp.where(vis, w, 0.0)
        ws.append(w.astype(BF16))
    accs = [accs[h] + jnp.dot(ws[h], vs[h], preferred_element_type=F32) for h in range(n)]
    carries = [carries[h] + jnp.sum(lr[h], axis=1, keepdims=True) for h in range(n)]
    return carries, accs


def _attn_prompt_kernel(q_ref, k_ref, v_ref, k_all_ref, v_all_ref, o_ref, ko_ref, vo_ref, *, heads, scale):
    del k_all_ref, v_all_ref
    tq = q_ref.shape[0]
    qi = pl.program_id(1)
    tri = _strict_upper(tq)
    cols = [slice(h * LANES, (h + 1) * LANES) for h in range(heads)]
    qs = [q_ref[:, cols[h]].astype(BF16) for h in range(heads)]
    own = pl.ds(pl.multiple_of(qi * tq, tq), tq)
    for h in range(heads):
        ko_ref[:, h, :] = k_ref[own, cols[h]]
        vo_ref[:, h, :] = v_ref[own, cols[h]]

    def all_heads(kb, carries, accs, masked):
        rows = pl.ds(pl.multiple_of(kb * tq, tq), tq)
        return _sb_blocks(qs, [k_ref[rows, cols[h]].astype(BF16) for h in range(heads)],
                          [v_ref[rows, cols[h]].astype(BF16) for h in range(heads)], tri, carries, accs,
                          masked=masked, scale=scale)

    carries, accs = all_heads(qi, [jnp.zeros((tq, 1), F32)] * heads, [jnp.zeros((tq, LANES), F32)] * heads, True)

    def cond(st):
        kb, carries, _ = st
        live = functools.reduce(jnp.maximum, carries)
        return jnp.logical_and(kb >= 0, jnp.max(live) > EXP_UNDERFLOW)

    def body(st):
        kb, carries, accs = st
        carries, accs = all_heads(kb, carries, accs, False)
        return kb - 1, carries, accs

    _, _, accs = lax.while_loop(cond, body, (qi - 1, carries, accs))
    for h in range(heads):
        o_ref[:, cols[h]] = accs[h]


def _attn_prompt(proj, k_all, v_all, layer, *, batch, seq, heads, scale):
    w = heads * LANES
    tq = _pick(seq, (128, 64))
    nq = seq // tq
    kv_spec = pl.BlockSpec((None, tq, heads, LANES), lambda b, i: (layer, b * nq + i, 0, 0))
    return pl.pallas_call(
        functools.partial(_attn_prompt_kernel, heads=heads, scale=scale),
        grid=(batch, nq),
        in_specs=[
            pl.BlockSpec((tq, w), lambda b, i: (b * nq + i, 0)),
            pl.BlockSpec((seq, w), lambda b, i: (b, 1)),
            pl.BlockSpec((seq, w), lambda b, i: (b, 2)),
            pl.BlockSpec(memory_space=pl.ANY),
            pl.BlockSpec(memory_space=pl.ANY),
        ],
        out_specs=[pl.BlockSpec((tq, w), lambda b, i: (b * nq + i, 0)), kv_spec, kv_spec],
        out_shape=[jax.ShapeDtypeStruct((batch * seq, w), F32),
                   jax.ShapeDtypeStruct(k_all.shape, F32), jax.ShapeDtypeStruct(v_all.shape, F32)],
        input_output_aliases={3: 1, 4: 2},
        name="attn_prompt",
        compiler_params=_params("parallel", "parallel"),
    )(proj, proj, proj, k_all, v_all)


def _attn_sample_kernel(q_ref, kn_ref, vn_ref, kc_ref, vc_ref, *rest, heads, tk, scale, emit_kv):
    if emit_kv:
        _, _, o_ref, cmax_ref, ko_ref, vo_ref, carry_ref, acc_ref = rest
    else:
        o_ref, cmax_ref, carry_ref, acc_ref = rest
    tq = q_ref.shape[0]
    tkp = kc_ref.shape[0]
    j = pl.program_id(1)

    def hs(h):
        return slice(h * LANES, (h + 1) * LANES)

    def qs():
        return [q_ref[:, hs(h)].astype(BF16) for h in range(heads)]

    def put(carries, accs):
        for h in range(heads):
            carry_ref[h] = jnp.broadcast_to(carries[h], (tq, LANES))
            acc_ref[:, hs(h)] = accs[h]

    @pl.when(j == 0)
    def _():
        put(*_sb_blocks(qs(), [kn_ref[:, hs(h)].astype(BF16) for h in range(heads)],
                        [vn_ref[:, hs(h)].astype(BF16) for h in range(heads)], _strict_upper(tq),
                        [jnp.zeros((tq, 1), F32)] * heads, [jnp.zeros((tq, LANES), F32)] * heads,
                        masked=True, scale=scale))
        if emit_kv:
            for h in range(heads):
                ko_ref[:, h, :] = kn_ref[:, hs(h)]
                vo_ref[:, h, :] = vn_ref[:, hs(h)]

    @pl.when(j > 0)
    def _():
        tri = _strict_upper(tk)
        for sb in reversed(range(tkp // tk)):
            live = functools.reduce(jnp.maximum, [carry_ref[h] for h in range(heads)])

            @pl.when(jnp.max(live) > EXP_UNDERFLOW)
            def _():
                rows = slice(sb * tk, (sb + 1) * tk)
                put(*_sb_blocks(qs(), [kc_ref[rows, h, :].astype(BF16) for h in range(heads)],
                                [vc_ref[rows, h, :].astype(BF16) for h in range(heads)], tri,
                                [carry_ref[h][:, 0:1] for h in range(heads)],
                                [acc_ref[:, hs(h)] for h in range(heads)], masked=False, scale=scale))

    @pl.when(j == pl.num_programs(1) - 1)
    def _():
        o_ref[...] = acc_ref[...]
        for h in range(heads):
            cmax_ref[h:h + 1, :] = jnp.max(carry_ref[h], axis=0, keepdims=True)


SAMPLE_CACHE_BLOCK = 512


def _attn_sample(proj, cache_k, cache_v, layer, kv_out, *, n_blocks, n_prompt, seq, scale):
    _, batch, past, heads, _ = cache_k.shape
    w = heads * LANES
    tkp = _pick(past, (SAMPLE_CACHE_BLOCK, 256, 128))
    tk = min(tkp, 256)
    last_block = past // tkp - 1
    row0 = n_prompt // seq
    emit_kv = kv_out is not None
    cache_spec = pl.BlockSpec((None, None, tkp, heads, LANES),
                              lambda b, j: (layer, b, last_block + 1 - jnp.maximum(j, 1), 0, 0))
    in_specs = [
        pl.BlockSpec((seq, w), lambda b, j: (row0 + b, 0)),
        pl.BlockSpec((seq, w), lambda b, j: (row0 + b, 1)),
        pl.BlockSpec((seq, w), lambda b, j: (row0 + b, 2)),
        cache_spec,
        cache_spec,
    ]
    out_specs = [
        pl.BlockSpec((seq, w), lambda b, j: (b, 0)),
        pl.BlockSpec((None, heads, LANES), lambda b, j: (b, 0, 0)),
    ]
    out_shape = [
        jax.ShapeDtypeStruct((batch * seq, w), F32),
        jax.ShapeDtypeStruct((batch, heads, LANES), F32),
    ]
    args = [proj, proj, proj, cache_k, cache_v]
    aliases = {}
    if emit_kv:
        kv_spec = pl.BlockSpec((None, seq, heads, LANES), lambda b, j: (layer, b, 0, 0))
        in_specs += [pl.BlockSpec(memory_space=pl.ANY)] * 2
        out_specs += [kv_spec, kv_spec]
        out_shape += [jax.ShapeDtypeStruct(a.shape, F32) for a in kv_out]
        args += list(kv_out)
        aliases = {5: 2, 6: 3}
    return pl.pallas_call(
        functools.partial(_attn_sample_kernel, heads=heads, tk=tk, scale=scale, emit_kv=emit_kv),
        grid=(batch, 1 + n_blocks),
        in_specs=in_specs,
        out_specs=out_specs,
        out_shape=out_shape,
        input_output_aliases=aliases,
        scratch_shapes=[pltpu.VMEM((heads, seq, LANES), F32), pltpu.VMEM((seq, w), F32)],
        name="attn_sample",
        compiler_params=_params("parallel", "arbitrary"),
    )(*args)


def _attn_sample_cached(proj, cache_k, cache_v, layer, k_all, v_all, *, n_prompt, seq, scale):
    past = cache_k.shape[2]
    kw = dict(n_prompt=n_prompt, seq=seq, scale=scale)
    n_all = past // _pick(past, (SAMPLE_CACHE_BLOCK, 256, 128))
    att, cmax, k_all, v_all = _attn_sample(proj, cache_k, cache_v, layer, (k_all, v_all), n_blocks=1, **kw)
    if n_all > 1:
        att = lax.cond(
            jnp.max(cmax) > EXP_UNDERFLOW,
            lambda ck, cv: _attn_sample(proj, ck, cv, layer, None, n_blocks=n_all, **kw)[0],
            lambda ck, cv: att,
            cache_k, cache_v)
    return att, k_all, v_all


def _mix_kernel(att_ref, bg_ref, cg_ref, xc_ref, cgh_ref, xch_ref, st_ref, wc_ref, na_ref, nc_ref,
                wo_ref, x_ref, g1_ref, o_ref, tail_ref, um1_ref, um2_ref, mix_ref,
                *, group, prompt_groups, groups_per_seq):
    tr, w = att_ref.shape
    gpt = tr // group
    i = pl.program_id(0)

    u = cg_ref[...] * xc_ref[...]
    uh = cgh_ref[...] * xch_ref[...]
    rows = lax.broadcasted_iota(jnp.int32, (tr, w), 0)
    h7 = uh[SUBLANES - 1:SUBLANES, :]
    h6 = uh[SUBLANES - 2:SUBLANES - 1, :]
    um1_ref[...] = jnp.where(rows == 0, h7, pltpu.roll(u, 1, axis=0))
    um2_ref[...] = jnp.where(rows == 0, h6, jnp.where(rows == 1, h7, pltpu.roll(u, 2, axis=0)))
    r8 = lax.broadcasted_iota(jnp.int32, (SUBLANES, w), 0)
    for g in range(gpt):
        gg = i * gpt + g
        starts_sequence = jnp.logical_or(gg >= prompt_groups, gg % groups_per_seq == 0)
        head = slice(g * group, g * group + SUBLANES)

        @pl.when(starts_sequence)
        def _():
            st0 = st_ref[g, 0:1, :]
            st1 = st_ref[g, 1:2, :]
            um1_ref[head, :] = jnp.where(r8 == 0, st1, um1_ref[head, :])
            um2_ref[head, :] = jnp.where(r8 == 0, st0, jnp.where(r8 == 1, st1, um2_ref[head, :]))

        tail_ref[g] = u[(g + 1) * group - SUBLANES:(g + 1) * group, :]

    conv = wc_ref[0:1, :] * um2_ref[...] + wc_ref[1:2, :] * um1_ref[...] + wc_ref[2:3, :] * u
    cv = bg_ref[...] * conv
    att = att_ref[...]
    for k in range(w // LANES):
        cs = slice(k * LANES, (k + 1) * LANES)
        a = att[:, cs]
        mix_ref[:, cs] = (a * lax.rsqrt(jnp.mean(a * a, axis=-1, keepdims=True) + EPS)
                          * na_ref[:, cs]).astype(BF16)
        c = cv[:, cs]
        mix_ref[:, w + k * LANES:w + (k + 1) * LANES] = (
            c * lax.rsqrt(jnp.mean(c * c, axis=-1, keepdims=True) + EPS) * nc_ref[:, cs]).astype(BF16)
    y = jnp.dot(mix_ref[...], wo_ref[...], preferred_element_type=F32)
    for g in range(gpt):
        rs = slice(g * group, (g + 1) * group)
        o_ref[rs, :] = x_ref[rs, :] + g1_ref[g] * y[rs]


def _mix_out(att, proj, state, w_conv, norm_att, norm_conv, w_out_bf16, x, g1, *, group, n_prompt, seq):
    n, d = x.shape
    w = att.shape[1]
    tr = _pick(n, (256, 128, 64))
    gpt = tr // group
    hb = tr // SUBLANES
    kern = functools.partial(_mix_kernel, group=group, prompt_groups=n_prompt // group,
                             groups_per_seq=seq // group)
    halo = lambda col: pl.BlockSpec((SUBLANES, w), lambda i: (jnp.maximum(i * hb - 1, 0), col))
    return pl.pallas_call(
        kern,
        grid=(n // tr,),
        in_specs=[
            pl.BlockSpec((tr, w), lambda i: (i, 0)),
            pl.BlockSpec((tr, w), lambda i: (i, 3)),
            pl.BlockSpec((tr, w), lambda i: (i, 4)),
            pl.BlockSpec((tr, w), lambda i: (i, 5)),
            halo(4),
            halo(5),
            pl.BlockSpec((gpt, CONV_K - 1, w), lambda i: (i, 0, 0)),
            pl.BlockSpec((CONV_K, w), lambda i: (0, 0)),
            pl.BlockSpec((1, w), lambda i: (0, 0)),
            pl.BlockSpec((1, w), lambda i: (0, 0)),
            pl.BlockSpec((2 * w, d), lambda i: (0, 0)),
            pl.BlockSpec((tr, d), lambda i: (i, 0)),
            pl.BlockSpec((gpt, 1, d), lambda i: (i, 0, 0)),
        ],
        out_specs=[
            pl.BlockSpec((tr, d), lambda i: (i, 0)),
            pl.BlockSpec((gpt, SUBLANES, w), lambda i: (i, 0, 0)),
        ],
        out_shape=[
            jax.ShapeDtypeStruct((n, d), F32),
            jax.ShapeDtypeStruct((n // group, SUBLANES, w), F32),
        ],
        scratch_shapes=[pltpu.VMEM((tr, w), F32), pltpu.VMEM((tr, w), F32), pltpu.VMEM((tr, 2 * w), BF16)],
        name="mix_out",
        compiler_params=_params("parallel"),
    )(att, proj, proj, proj, proj, proj, state, w_conv, norm_att.reshape(1, w), norm_conv.reshape(1, w),
      w_out_bf16, x, g1)


def _staircase():
    return [(r, c) for r in range(PEER_TOPK) for c in range(PEER_TOPK) if (r + 1) * (c + 1) <= PEER_TOPK]


def _sort16_network():
    pairs, n, p = [], 16, 1
    while p < n:
        k = p
        while k >= 1:
            for j in range(k % p, n - k, 2 * k):
                for i in range(min(k, n - j - k)):
                    if (i + j) // (2 * p) == (i + j + k) // (2 * p):
                        pairs.append((i + j, i + j + k))
            k //= 2
        p *= 2
    return pairs


def _top16_values(slabs):
    x = list(slabs)
    for i, j in _sort16_network():
        x[i], x[j] = jnp.maximum(x[i], x[j]), jnp.minimum(x[i], x[j])
    for shift in (4, 2, 1):
        z = [jnp.maximum(x[r], pltpu.roll(x[PEER_TOPK - 1 - r], shift, axis=0)) for r in range(PEER_TOPK)]
        for dist in (8, 4, 2, 1):
            for i in range(PEER_TOPK):
                if i & dist == 0:
                    z[i], z[i + dist] = jnp.maximum(z[i], z[i + dist]), jnp.minimum(z[i], z[i + dist])
        x = z
    return x


def _top16_ranked(s, a_ref, slot):
    nk, tt = s.shape
    kidx = lax.broadcasted_iota(jnp.int32, (nk, tt), 0).astype(F32)
    work = s
    rank = jnp.full((nk, tt), float(PEER_TOPK), F32)
    for r in range(PEER_TOPK):
        m = jnp.max(work, axis=0, keepdims=True)
        first = jnp.min(jnp.where(work == m, kidx, float(nk)), axis=0, keepdims=True)
        hit = kidx == first
        rank = jnp.where(hit, float(r), rank)
        work = jnp.where(hit, -jnp.inf, work)
        a_ref[r, pl.ds(slot, 1), :] = m
    return rank


def _kept_per_row(a_ref, l_ref, heads):
    cands = _staircase()
    a0 = [a_ref[r, 0:heads, :] for r in range(PEER_TOPK)]
    a1 = [a_ref[c, heads:2 * heads, :] for c in range(PEER_TOPK)]
    sums = [a0[r] + a1[c] for r, c in cands]
    ahead = [0.0] * len(cands)
    for xi, (rx, cx) in enumerate(cands):
        for yi in range(xi):
            ry, cy = cands[yi]
            if ry <= rx and cy <= cx:
                ahead[xi] = ahead[xi] + 1.0
            else:
                y_first = sums[yi] >= sums[xi]
                ahead[xi] = ahead[xi] + jnp.where(y_first, 1.0, 0.0)
                ahead[yi] = ahead[yi] + jnp.where(y_first, 0.0, 1.0)
    zsum = jnp.zeros_like(sums[0])
    kept = [jnp.zeros_like(sums[0]) for _ in range(PEER_TOPK)]
    for xi, (rx, cx) in enumerate(cands):
        keep = jnp.where(ahead[xi] < float(PEER_TOPK), 1.0, 0.0)
        kept[rx] = kept[rx] + keep
        zsum = zsum + keep * jnp.exp(sums[xi] - sums[0])
    for r in range(PEER_TOPK):
        l_ref[r] = kept[r]
    l_ref[PEER_TOPK] = 1.0 / zsum


def _peer_select_kernel(q_ref, key_ref, ea_ref, lr_ref, eb_ref, cr_ref, s_ref, rk_ref, a_ref, l_ref,
                        *, heads):
    nk, tt = s_ref.shape[1], s_ref.shape[2]
    nslab = nk // SUBLANES
    ambiguous = jnp.zeros((1, tt), F32)
    for hp in range(2 * heads):
        slot = (hp % 2) * heads + hp // 2
        s = lax.dot_general(key_ref[hp], q_ref[:, hp * LANES:(hp + 1) * LANES], (((1,), (1,)), ((), ())),
                            precision=lax.Precision.HIGHEST, preferred_element_type=F32)
        s_ref[slot] = s
        slabs = [s[v * SUBLANES:(v + 1) * SUBLANES, :] for v in range(nslab)]
        best = _top16_values(slabs)
        for r in range(PEER_TOPK):
            a_ref[r, slot:slot + 1, :] = best[r][0:1, :]
        at_least_last = jnp.zeros((SUBLANES, tt), F32)
        for v in range(nslab):
            at_least_last = at_least_last + jnp.where(slabs[v] >= best[PEER_TOPK - 1], 1.0, 0.0)
        bad = jnp.sum(at_least_last, axis=0, keepdims=True) != float(PEER_TOPK)
        for r in range(PEER_TOPK - 1):
            bad = jnp.logical_or(bad, best[r][0:1, :] == best[r + 1][0:1, :])
        ambiguous = jnp.maximum(ambiguous, jnp.where(bad, 1.0, 0.0))
    redo = jnp.max(ambiguous) > 0.0

    @pl.when(redo)
    def _():
        def one_slot(slot, carry):
            rk_ref[slot] = _top16_ranked(s_ref[slot], a_ref, slot)
            return carry
        lax.fori_loop(0, 2 * heads, one_slot, 0)

    _kept_per_row(a_ref, l_ref, heads)

    for h in range(heads):
        ea_ref[h] = jnp.exp(s_ref[h] - a_ref[0, h:h + 1, :]) * l_ref[PEER_TOPK, h:h + 1, :]
        eb_ref[h] = jnp.exp(s_ref[heads + h] - a_ref[0, heads + h:heads + h + 1, :]).astype(eb_ref.dtype)

    @pl.when(jnp.logical_not(redo))
    def _():
        for h in range(heads):
            s0, s1 = s_ref[h], s_ref[heads + h]
            lrow = jnp.zeros((nk, tt), F32)
            crank = jnp.full((nk, tt), float(PEER_TOPK), F32)
            for r in range(PEER_TOPK):
                lrow = jnp.where(s0 == a_ref[r, h:h + 1, :], l_ref[r, h:h + 1, :], lrow)
                crank = jnp.where(s1 == a_ref[r, heads + h:heads + h + 1, :], float(r), crank)
            lr_ref[h] = lrow
            cr_ref[h] = crank.astype(cr_ref.dtype)

    @pl.when(redo)
    def _():
        for h in range(heads):
            rk0 = rk_ref[h]
            lrow = jnp.zeros((nk, tt), F32)
            for r in range(PEER_TOPK):
                lrow = jnp.where(rk0 == float(r), l_ref[r, h:h + 1, :], lrow)
            lr_ref[h] = lrow
            cr_ref[h] = rk_ref[heads + h].astype(cr_ref.dtype)


def _peer_select(q, keys):
    n, dq = q.shape
    heads, _, nk, dk = keys.shape
    assert nk == PEER_TOPK * SUBLANES and dk == LANES
    tt = _pick(n, (256, 128))
    out = jax.ShapeDtypeStruct((heads, nk, n), F32)
    out_j = jax.ShapeDtypeStruct((heads, nk, n), BF16)
    ospec = pl.BlockSpec((heads, nk, tt), lambda t: (0, 0, t))
    return pl.pallas_call(
        functools.partial(_peer_select_kernel, heads=heads),
        grid=(n // tt,),
        in_specs=[
            pl.BlockSpec((tt, dq), lambda t: (t, 0)),
            pl.BlockSpec((2 * heads, nk, dk), lambda t: (0, 0, 0)),
        ],
        out_specs=[ospec, ospec, ospec, ospec],
        out_shape=[out, out, out_j, out_j],
        scratch_shapes=[
            pltpu.VMEM((2 * heads, nk, tt), F32),
            pltpu.VMEM((2 * heads, nk, tt), F32),
            pltpu.VMEM((PEER_TOPK, 2 * heads, tt), F32),
            pltpu.VMEM((PEER_TOPK + 1, heads, tt), F32),
        ],
        name="peer_select",
        compiler_params=_params("parallel"),
    )(q, keys.reshape(2 * heads, nk, dk))


PEER_CHUNK = 1024


def _peer_dense_kernel(x_ref, g2_ref, h_ref, u_ref, vt_ref, vt_last_ref, ea_ref, lr_ref, eb_ref, cr_ref, o_ref,
                       acc_ref, act_ref, coef_lo_ref, coef_hi_ref, *, heads, group):
    tt = act_ref.shape[1]
    c = pl.program_id(1)
    one_trip = jnp.minimum(c, 0) + 1
    iph = PEER_CHUNK // 2 // LANES

    @pl.when(jnp.logical_and(pl.program_id(0) == 0, c == 0))
    def _():
        coef_hi_ref[...] = jnp.zeros_like(coef_hi_ref)

    @pl.when(c == 0)
    def _():
        acc_ref[...] = jnp.zeros_like(acc_ref)

    hrows = PEER_CHUNK // 2

    def first_matmul(half):
        rows = slice(half * hrows, (half + 1) * hrows)
        act_ref[rows, :] = lax.dot_general(u_ref[rows, :], h_ref[...], (((1,), (1,)), ((), ())),
                                           preferred_element_type=F32)

    first_matmul(0)

    def build(half, dst):
        tb = min(tt, 2 * LANES)
        for s in range(iph):
            il = half * iph + s
            rows = slice(il * LANES, (il + 1) * LANES)
            for t0 in range(0, tt, tb):
                cols = slice(t0, t0 + tb)
                wsum = jnp.zeros((LANES, tb), BF16)
                for h in range(heads):
                    lrow = lr_ref[h, il:il + 1, cols].astype(BF16)
                    ea = ea_ref[h, il:il + 1, cols].astype(BF16)
                    wsum = wsum + jnp.where(cr_ref[h, :, cols] < lrow, eb_ref[h, :, cols] * ea,
                                            jnp.zeros((), BF16))
                g = jax.nn.gelu(act_ref[rows, cols], approximate=True)
                dst[s * LANES:(s + 1) * LANES, cols] = (wsum.astype(F32) * g).astype(BF16)

    def region(half, src, dst):
        def body(_, carry):
            build(half, dst)
            acc_ref[...] += jnp.dot(vt_ref[half], src[...], preferred_element_type=F32)
            if half == 0:
                first_matmul(1)
            return carry
        lax.fori_loop(0, one_trip, body, 0)

    region(0, coef_hi_ref, coef_lo_ref)
    region(1, coef_lo_ref, coef_hi_ref)

    @pl.when(c == pl.num_programs(1) - 1)
    def _():
        acc_ref[...] += jnp.dot(vt_last_ref[0], coef_hi_ref[...], preferred_element_type=F32)
        y = acc_ref[...].T
        for g in range(tt // group):
            rs = slice(g * group, (g + 1) * group)
            o_ref[rs, :] = x_ref[rs, :] + g2_ref[g] * y[rs]


def _peer_dense(x, g2, h_bf16, u_bf16, v_bf16, ea, lr, eb, cr, *, group):
    n, d = h_bf16.shape
    e = u_bf16.shape[0]
    heads, nk, _ = ea.shape
    assert e % PEER_CHUNK == 0
    tt = _pick(n, (512, 256, 128))
    half = PEER_CHUNK // 2
    nhalves = e // half
    vt = v_bf16.reshape(nhalves, half, d).transpose(0, 2, 1)
    vt = jnp.concatenate([jnp.zeros((1, d, half), BF16), vt], axis=0)
    sel_i = pl.BlockSpec((heads, PEER_CHUNK // LANES, tt), lambda t, c: (0, c, t))
    sel_j = pl.BlockSpec((heads, nk, tt), lambda t, c: (0, 0, t))
    return pl.pallas_call(
        functools.partial(_peer_dense_kernel, heads=heads, group=group),
        grid=(n // tt, e // PEER_CHUNK),
        in_specs=[
            pl.BlockSpec((tt, d), lambda t, c: (t, 0)),
            pl.BlockSpec((tt // group, 1, d), lambda t, c: (t, 0, 0)),
            pl.BlockSpec((tt, d), lambda t, c: (t, 0)),
            pl.BlockSpec((PEER_CHUNK, d), lambda t, c: (c, 0)),
            pl.BlockSpec((2, d, half), lambda t, c: (c, 0, 0)),
            pl.BlockSpec((1, d, half), lambda t, c: (nhalves, 0, 0)),
            sel_i, sel_i, sel_j, sel_j,
        ],
        out_specs=pl.BlockSpec((tt, d), lambda t, c: (t, 0)),
        out_shape=jax.ShapeDtypeStruct((n, d), F32),
        scratch_shapes=[pltpu.VMEM((d, tt), F32), pltpu.VMEM((PEER_CHUNK, tt), F32),
                        pltpu.VMEM((half, tt), BF16), pltpu.VMEM((half, tt), BF16)],
        name="peer_dense",
        compiler_params=_params("arbitrary", "arbitrary"),
    )(x, g2, h_bf16, u_bf16, vt, vt, ea, lr, eb, cr)


def _final_norm_kernel(x_ref, g_ref, o_ref):
    x = x_ref[...]
    o_ref[...] = x * lax.rsqrt(jnp.mean(x * x, axis=-1, keepdims=True) + EPS) * g_ref[...]


def _final_norm(x, gain):
    n, d = x.shape
    tr = _pick(n, (512, 256, 128, 64))
    row = pl.BlockSpec((tr, d), lambda i: (i, 0))
    return pl.pallas_call(
        _final_norm_kernel,
        grid=(n // tr,),
        in_specs=[row, pl.BlockSpec((1, d), lambda i: (0, 0))],
        out_specs=row,
        out_shape=jax.ShapeDtypeStruct((n, d), F32),
        compiler_params=_params("parallel"),
    )(x, gain.reshape(1, d))


def kernel(x_prompt, x_sample, cache_k, cache_v, state_conv, c_prompt, c_sample, w_ada, b_ada, norm_mix,
           w_in, w_conv, norm_att_out, norm_conv_out, w_out, norm_ffn, w_peer_q, peer_keys, peer_u, peer_v,
           norm_final):
    bp, tp, d = x_prompt.shape
    bs, ts, _ = x_sample.shape
    depth = w_in.shape[0]
    past = cache_k.shape[2]
    heads = cache_k.shape[3]
    w = heads * LANES
    assert cache_k.shape[4] == LANES and state_conv.shape[-1] == w and w_in.shape[2] == 6 * w
    assert tp % ts == 0 and ts % SUBLANES == 0
    group = ts
    n_p, n_s = bp * tp, bs * ts
    n = n_p + n_s
    scale = float(LANES) ** -0.5

    x = jnp.concatenate([x_prompt.reshape(n_p, d), x_sample.reshape(n_s, d)], axis=0)

    c_all = jnp.concatenate([c_prompt, c_sample], axis=0)
    r = c_all.shape[0]
    r_pad = -(-r // SUBLANES) * SUBLANES
    mod = _ada(jnp.pad(c_all, ((0, r_pad - r), (0, 0))), w_ada, b_ada)
    group_row = jnp.concatenate([jnp.repeat(jnp.arange(bp), tp // group), bp + jnp.arange(bs)])
    mod = jnp.take(mod, group_row, axis=1).reshape(depth, n // group, 6, 1, d)

    state =jnp.concatenate([jnp.zeros((depth, n_p // group, CONV_K - 1, w), F32), state_conv], axis=1)
    seq_end_group = jnp.concatenate([(jnp.arange(bp) + 1) * (tp // group) - 1, n_p // group + jnp.arange(bs)])

    kp = jnp.zeros((depth, n_p, heads, LANES), F32)
    vp = jnp.zeros((depth, n_p, heads, LANES), F32)
    ks = jnp.zeros((depth, n_s, heads, LANES), F32)
    vs = jnp.zeros((depth, n_s, heads, LANES), F32)
    cp, cs = [], []
    for l in range(depth):
        sh1, sc1, g1, sh2, sc2, g2 = (mod[l, :, k] for k in range(6))
        proj = _norm_mod_matmul(x, norm_mix[l], sc1, sh1, w_in[l].astype(BF16), group=group, emit_h=False)
        att_p, kp, vp = _attn_prompt(proj, kp, vp, l, batch=bp, seq=tp, heads=heads, scale=scale)
        att_s, ks, vs = _attn_sample_cached(proj, cache_k, cache_v, l, ks, vs, n_prompt=n_p, seq=ts, scale=scale)
        att = jnp.concatenate([att_p, att_s], axis=0)
        x, tails = _mix_out(att, proj, state[l], w_conv[l], norm_att_out[l], norm_conv_out[l],
                            w_out[l].astype(BF16), x, g1, group=group, n_prompt=n_p, seq=tp)
        q, h2 = _norm_mod_matmul(x, norm_ffn[l], sc2, sh2, w_peer_q[l].astype(BF16), group=group, emit_h=True)
        ea, lr, eb, cr = _peer_select(q, peer_keys[l])
        x = _peer_dense(x, g2, h2, peer_u[l].astype(BF16), peer_v[l].astype(BF16), ea, lr, eb, cr, group=group)

        conv_new = jnp.take(tails, seq_end_group, axis=0)[:, SUBLANES - (CONV_K - 1):, :]
        cp.append(conv_new[:bp])
        cs.append(conv_new[bp:])

    y = _final_norm(x, norm_final)
    return (y[:n_p].reshape(bp, tp, d), y[n_p:].reshape(bs, ts, d),
            kp.reshape(depth, bp, tp, heads, LANES), vp.reshape(depth, bp, tp, heads, LANES), jnp.stack(cp),
            ks.reshape(depth, bs, ts, heads, LANES), vs.reshape(depth, bs, ts, heads, LANES), jnp.stack(cs))
```

```python
import functools
import math

import jax
import jax.numpy as jnp
from jax import lax
from jax.experimental import pallas as pl
from jax.experimental.pallas import tpu as pltpu

F32 = jnp.float32
BF16 = jnp.bfloat16
EPS = 1e-6
LANES = 128
SUBLANES = 8
CONV_K = 3
PEER_TOPK = 16
VMEM_LIMIT_BYTES = 56 * 1024 * 1024
EXP_UNDERFLOW = -104.0


def _params(*sem):
    return pltpu.CompilerParams(dimension_semantics=sem, vmem_limit_bytes=VMEM_LIMIT_BYTES)


def _pick(n, cands):
    for c in cands:
        if n % c == 0:
            return c
    raise ValueError(f"no tile in {cands} divides {n}")


def _ada_kernel(c_ref, w_ref, b_ref, o_ref):
    c = c_ref[...]
    s = (c * jax.nn.sigmoid(c)).astype(BF16)
    o_ref[...] = jnp.dot(s, w_ref[...].astype(BF16), preferred_element_type=F32) + b_ref[...]


def _ada(c_all, w_ada, b_ada):
    depth, d, n6 = w_ada.shape
    r = c_all.shape[0]
    tn = _pick(n6, (1024, 512, 256, 128))
    return pl.pallas_call(
        _ada_kernel,
        grid=(depth, n6 // tn),
        in_specs=[
            pl.BlockSpec((r, d), lambda l, j: (0, 0)),
            pl.BlockSpec((None, d, tn), lambda l, j: (l, 0, j)),
            pl.BlockSpec((None, 1, tn), lambda l, j: (l, 0, j)),
        ],
        out_specs=pl.BlockSpec((None, r, tn), lambda l, j: (l, 0, j)),
        out_shape=jax.ShapeDtypeStruct((depth, r, n6), F32),
        name="ada",
        compiler_params=_params("parallel", "parallel"),
    )(c_all, w_ada, b_ada.reshape(depth, 1, n6))


def _nmm_kernel(x_ref, g_ref, sc_ref, sh_ref, w_ref, o_ref, *rest, group, emit_h):
    hs_ref = rest[-1]
    tr = x_ref.shape[0]

    @pl.when(pl.program_id(1) == 0)
    def _():
        x = x_ref[...]
        y = x * lax.rsqrt(jnp.mean(x * x, axis=-1, keepdims=True) + EPS) * g_ref[...]
        for g in range(tr // group):
            rows = slice(g * group, (g + 1) * group)
            hs_ref[rows, :] = (y[rows] * (1.0 + sc_ref[g]) + sh_ref[g]).astype(BF16)
        if emit_h:
            rest[0][...] = hs_ref[...]

    o_ref[...] = jnp.dot(hs_ref[...], w_ref[...], preferred_element_type=F32)


def _norm_mod_matmul(x, gain, sc, sh, w_bf16, *, group, emit_h):
    n, d = x.shape
    nout = w_bf16.shape[1]
    tr = _pick(n, (512, 256, 128, 64))
    tn = _pick(nout, (2048, 1024, 512, 256, 128))
    gpt = tr // group
    out_shape = [jax.ShapeDtypeStruct((n, nout), F32)]
    out_specs = [pl.BlockSpec((tr, tn), lambda i, j: (i, j))]
    if emit_h:
        out_shape.append(jax.ShapeDtypeStruct((n, d), BF16))
        out_specs.append(pl.BlockSpec((tr, d), lambda i, j: (i, 0)))
    res = pl.pallas_call(
        functools.partial(_nmm_kernel, group=group, emit_h=emit_h),
        grid=(n // tr, nout // tn),
        in_specs=[
            pl.BlockSpec((tr, d), lambda i, j: (i, 0)),
            pl.BlockSpec((1, d), lambda i, j: (0, 0)),
            pl.BlockSpec((gpt, 1, d), lambda i, j: (i, 0, 0)),
            pl.BlockSpec((gpt, 1, d), lambda i, j: (i, 0, 0)),
            pl.BlockSpec((d, tn), lambda i, j: (0, j)),
        ],
        out_specs=out_specs,
        out_shape=out_shape,
        scratch_shapes=[pltpu.VMEM((tr, d), BF16)],
        name="norm_mod_matmul",
        compiler_params=_params("parallel", "arbitrary"),
    )(x, gain.reshape(1, d), sc, sh, w_bf16)
    return res if emit_h else res[0]


def _strict_upper(n):
    return (lax.broadcasted_iota(jnp.int32, (n, n), 0) > lax.broadcasted_iota(jnp.int32, (n, n), 1)).astype(BF16)


def _sb_blocks(qs, ks, vs, tri, carries, accs, *, masked, scale):
    n = len(qs)
    tq = qs[0].shape[0]
    zs = [lax.dot_general(qs[h], ks[h], (((1,), (1,)), ((), ())), preferred_element_type=F32) * scale
          for h in range(n)]
    ls = [jnp.minimum(z, 0.0) - jnp.log1p(jnp.exp(-jnp.abs(z))) for z in zs]
    lr = [l - z for l, z in zip(ls, zs)]
    if masked:
        shape = zs[0].shape
        vis = lax.broadcasted_iota(jnp.int32, shape, 1) < lax.broadcasted_iota(jnp.int32, shape, 0)
        lr = [jnp.where(vis, x, 0.0) for x in lr]
    lr_all = jnp.concatenate(lr, axis=0)
    hi = lr_all.astype(BF16)
    r1 = lr_all - hi.astype(F32)
    mid = r1.astype(BF16)
    lo = (r1 - mid.astype(F32)).astype(BF16)
    suf = (jnp.dot(hi, tri, preferred_element_type=F32) + jnp.dot(mid, tri, preferred_element_type=F32)
           + jnp.dot(lo, tri, preferred_element_type=F32))
    ws = []
    for h in range(n):
        w = jnp.exp(ls[h] + suf[h * tq:(h + 1) * tq] + carries[h])
        if masked:
            w = jnp.where(vis, w, 0.0)
        ws.append(w.astype(BF16))
    accs = [accs[h] + jnp.dot(ws[h], vs[h], preferred_element_type=F32) for h in range(n)]
    carries = [carries[h] + jnp.sum(lr[h], axis=1, keepdims=True) for h in range(n)]
    return carries, accs


def _attn_prompt_kernel(q_ref, k_ref, v_ref, *rest, heads, scale):
    o_ref, ko_ref, vo_ref = rest[-3:]
    tq = q_ref.shape[0]
    qi = pl.program_id(1)
    tri = _strict_upper(tq)
    cols = [slice(h * LANES, (h + 1) * LANES) for h in range(heads)]
    qs = [q_ref[:, cols[h]].astype(BF16) for h in range(heads)]
    own = pl.ds(pl.multiple_of(qi * tq, tq), tq)
    for h in range(heads):
        ko_ref[:, h, :] = k_ref[own, cols[h]]
        vo_ref[:, h, :] = v_ref[own, cols[h]]

    def all_heads(kb, carries, accs, masked):
        rows = pl.ds(pl.multiple_of(kb * tq, tq), tq)
        return _sb_blocks(qs, [k_ref[rows, cols[h]].astype(BF16) for h in range(heads)],
                          [v_ref[rows, cols[h]].astype(BF16) for h in range(heads)], tri, carries, accs,
                          masked=masked, scale=scale)

    carries, accs = all_heads(qi, [jnp.zeros((tq, 1), F32)] * heads, [jnp.zeros((tq, LANES), F32)] * heads, True)

    def cond(st):
        kb, carries, _ = st
        live = functools.reduce(jnp.maximum, carries)
        return jnp.logical_and(kb >= 0, jnp.max(live) > EXP_UNDERFLOW)

    def body(st):
        kb, carries, accs = st
        carries, accs = all_heads(kb, carries, accs, False)
        return kb - 1, carries, accs

    _, _, accs = lax.while_loop(cond, body, (qi - 1, carries, accs))
    for h in range(heads):
        o_ref[:, cols[h]] = accs[h]


def _attn_prompt(proj, k_all, v_all, layer, *, batch, seq, heads, scale):
    w = heads * LANES
    tq = _pick(seq, (128, 64))
    nq = seq // tq
    kv_spec = pl.BlockSpec((None, tq, heads, LANES), lambda b, i: (layer, b * nq + i, 0, 0))
    return pl.pallas_call(
        functools.partial(_attn_prompt_kernel, heads=heads, scale=scale),
        grid=(batch, nq),
        in_specs=[
            pl.BlockSpec((tq, w), lambda b, i: (b * nq + i, 0)),
            pl.BlockSpec((seq, w), lambda b, i: (b, 1)),
            pl.BlockSpec((seq, w), lambda b, i: (b, 2)),
            pl.BlockSpec(memory_space=pl.ANY),
            pl.BlockSpec(memory_space=pl.ANY),
        ],
        out_specs=[pl.BlockSpec((tq, w), lambda b, i: (b * nq + i, 0)), kv_spec, kv_spec],
        out_shape=[jax.ShapeDtypeStruct((batch * seq, w), F32),
                   jax.ShapeDtypeStruct(k_all.shape, F32), jax.ShapeDtypeStruct(v_all.shape, F32)],
        input_output_aliases={3: 1, 4: 2},
        name="attn_prompt",
        compiler_params=_params("parallel", "parallel"),
    )(proj, proj, proj, k_all, v_all)


def _attn_sample_kernel(q_ref, kn_ref, vn_ref, kc_ref, vc_ref, *rest, heads, tk, scale, emit_kv):
    if emit_kv:
        _, _, o_ref, cmax_ref, ko_ref, vo_ref, carry_ref, acc_ref = rest
    else:
        o_ref, cmax_ref, carry_ref, acc_ref = rest
    tq = q_ref.shape[0]
    tkp = kc_ref.shape[0]
    j = pl.program_id(1)

    def hs(h):
        return slice(h * LANES, (h + 1) * LANES)

    def qs():
        return [q_ref[:, hs(h)].astype(BF16) for h in range(heads)]

    def put(carries, accs):
        for h in range(heads):
            carry_ref[h] = jnp.broadcast_to(carries[h], (tq, LANES))
            acc_ref[:, hs(h)] = accs[h]

    @pl.when(j == 0)
    def _():
        put(*_sb_blocks(qs(), [kn_ref[:, hs(h)].astype(BF16) for h in range(heads)],
                        [vn_ref[:, hs(h)].astype(BF16) for h in range(heads)], _strict_upper(tq),
                        [jnp.zeros((tq, 1), F32)] * heads, [jnp.zeros((tq, LANES), F32)] * heads,
                        masked=True, scale=scale))
        if emit_kv:
            for h in range(heads):
                ko_ref[:, h, :] = kn_ref[:, hs(h)]
                vo_ref[:, h, :] = vn_ref[:, hs(h)]

    @pl.when(j > 0)
    def _():
        tri = _strict_upper(tk)
        for sb in reversed(range(tkp // tk)):
            live = functools.reduce(jnp.maximum, [carry_ref[h] for h in range(heads)])

            @pl.when(jnp.max(live) > EXP_UNDERFLOW)
            def _():
                rows = slice(sb * tk, (sb + 1) * tk)
                put(*_sb_blocks(qs(), [kc_ref[rows, h, :].astype(BF16) for h in range(heads)],
                                [vc_ref[rows, h, :].astype(BF16) for h in range(heads)], tri,
                                [carry_ref[h][:, 0:1] for h in range(heads)],
                                [acc_ref[:, hs(h)] for h in range(heads)], masked=False, scale=scale))

    @pl.when(j == pl.num_programs(1) - 1)
    def _():
        o_ref[...] = acc_ref[...]
        for h in range(heads):
            cmax_ref[h:h + 1, :] = jnp.max(carry_ref[h], axis=0, keepdims=True)


SAMPLE_CACHE_BLOCK = 512


def _attn_sample(proj, cache_k, cache_v, layer, kv_out, *, n_blocks, n_prompt, seq, scale):
    _, batch, past, heads, _ = cache_k.shape
    w = heads * LANES
    tkp = _pick(past, (SAMPLE_CACHE_BLOCK, 256, 128))
    tk = min(tkp, 256)
    last_block = past // tkp - 1
    row0 = n_prompt // seq
    emit_kv = kv_out is not None
    cache_spec = pl.BlockSpec((None, None, tkp, heads, LANES),
                              lambda b, j: (layer, b, last_block + 1 - jnp.maximum(j, 1), 0, 0))
    in_specs = [
        pl.BlockSpec((seq, w), lambda b, j: (row0 + b, 0)),
        pl.BlockSpec((seq, w), lambda b, j: (row0 + b, 1)),
        pl.BlockSpec((seq, w), lambda b, j: (row0 + b, 2)),
        cache_spec,
        cache_spec,
    ]
    out_specs = [
        pl.BlockSpec((seq, w), lambda b, j: (b, 0)),
        pl.BlockSpec((None, heads, LANES), lambda b, j: (b, 0, 0)),
    ]
    out_shape = [
        jax.ShapeDtypeStruct((batch * seq, w), F32),
        jax.ShapeDtypeStruct((batch, heads, LANES), F32),
    ]
    args = [proj, proj, proj, cache_k, cache_v]
    aliases = {}
    if emit_kv:
        kv_spec = pl.BlockSpec((None, seq, heads, LANES), lambda b, j: (layer, b, 0, 0))
        in_specs += [pl.BlockSpec(memory_space=pl.ANY)] * 2
        out_specs += [kv_spec, kv_spec]
        out_shape += [jax.ShapeDtypeStruct(a.shape, F32) for a in kv_out]
        args += list(kv_out)
        aliases = {5: 2, 6: 3}
    return pl.pallas_call(
        functools.partial(_attn_sample_kernel, heads=heads, tk=tk, scale=scale, emit_kv=emit_kv),
        grid=(batch, 1 + n_blocks),
        in_specs=in_specs,
        out_specs=out_specs,
        out_shape=out_shape,
        input_output_aliases=aliases,
        scratch_shapes=[pltpu.VMEM((heads, seq, LANES), F32), pltpu.VMEM((seq, w), F32)],
        name="attn_sample",
        compiler_params=_params("parallel", "arbitrary"),
    )(*args)


def _attn_sample_cached(proj, cache_k, cache_v, layer, k_all, v_all, *, n_prompt, seq, scale):
    past = cache_k.shape[2]
    kw = dict(n_prompt=n_prompt, seq=seq, scale=scale)
    n_all = past // _pick(past, (SAMPLE_CACHE_BLOCK, 256, 128))
    att, cmax, k_all, v_all = _attn_sample(proj, cache_k, cache_v, layer, (k_all, v_all), n_blocks=1, **kw)
    if n_all > 1:
        att = lax.cond(
            jnp.max(cmax) > EXP_UNDERFLOW,
            lambda ck, cv: _attn_sample(proj, ck, cv, layer, None, n_blocks=n_all, **kw)[0],
            lambda ck, cv: att,
            cache_k, cache_v)
    return att, k_all, v_all


def _mix_kernel(att_p_ref, att_s_ref, bg_ref, cg_ref, xc_ref, cgh_ref, xch_ref, st_ref, wc_ref, na_ref, nc_ref,
                wo_ref, x_ref, g1_ref, o_ref, tail_ref, um1_ref, um2_ref, mix_ref,
                *, group, prompt_groups, groups_per_seq):
    tr, w = bg_ref.shape
    gpt = tr // group
    i = pl.program_id(0)

    u = cg_ref[...] * xc_ref[...]
    uh = cgh_ref[...] * xch_ref[...]
    rows = lax.broadcasted_iota(jnp.int32, (tr, w), 0)
    h7 = uh[SUBLANES - 1:SUBLANES, :]
    h6 = uh[SUBLANES - 2:SUBLANES - 1, :]
    um1_ref[...] = jnp.where(rows == 0, h7, pltpu.roll(u, 1, axis=0))
    um2_ref[...] = jnp.where(rows == 0, h6, jnp.where(rows == 1, h7, pltpu.roll(u, 2, axis=0)))
    r8 = lax.broadcasted_iota(jnp.int32, (SUBLANES, w), 0)
    for g in range(gpt):
        gg = i * gpt + g
        starts_sequence = jnp.logical_or(gg >= prompt_groups, gg % groups_per_seq == 0)
        head = slice(g * group, g * group + SUBLANES)

        @pl.when(starts_sequence)
        def _():
            st0 = st_ref[g, 0:1, :]
            st1 = st_ref[g, 1:2, :]
            um1_ref[head, :] = jnp.where(r8 == 0, st1, um1_ref[head, :])
            um2_ref[head, :] = jnp.where(r8 == 0, st0, jnp.where(r8 == 1, st1, um2_ref[head, :]))

        tail_ref[g] = u[(g + 1) * group - SUBLANES:(g + 1) * group, :]

    conv = wc_ref[0:1, :] * um2_ref[...] + wc_ref[1:2, :] * um1_ref[...] + wc_ref[2:3, :] * u
    cv = bg_ref[...] * conv
    att = jnp.where(i * gpt < prompt_groups, att_p_ref[...], att_s_ref[...])
    for k in range(w // LANES):
        cs = slice(k * LANES, (k + 1) * LANES)
        a = att[:, cs]
        mix_ref[:, cs] = (a * lax.rsqrt(jnp.mean(a * a, axis=-1, keepdims=True) + EPS)
                          * na_ref[:, cs]).astype(BF16)
        c = cv[:, cs]
        mix_ref[:, w + k * LANES:w + (k + 1) * LANES] = (
            c * lax.rsqrt(jnp.mean(c * c, axis=-1, keepdims=True) + EPS) * nc_ref[:, cs]).astype(BF16)
    y = jnp.dot(mix_ref[...], wo_ref[...], preferred_element_type=F32)
    for g in range(gpt):
        rs = slice(g * group, (g + 1) * group)
        o_ref[rs, :] = x_ref[rs, :] + g1_ref[g] * y[rs]


def _mix_out(att_p, att_s, proj, state, w_conv, norm_att, norm_conv, w_out_bf16, x, g1, *, group, n_prompt, seq):
    n, d = x.shape
    w = att_p.shape[1]
    tr = _pick(math.gcd(n_prompt, n - n_prompt), (256, 128, 64))
    gpt = tr // group
    hb = tr // SUBLANES
    ptiles = n_prompt // tr
    kern = functools.partial(_mix_kernel, group=group, prompt_groups=n_prompt // group,
                             groups_per_seq=seq // group)
    halo = lambda col: pl.BlockSpec((SUBLANES, w), lambda i: (jnp.maximum(i * hb - 1, 0), col))
    return pl.pallas_call(
        kern,
        grid=(n // tr,),
        in_specs=[
            pl.BlockSpec((tr, w), lambda i: (jnp.minimum(i, ptiles - 1), 0)),
            pl.BlockSpec((tr, w), lambda i: (jnp.maximum(i - ptiles, 0), 0)),
            pl.BlockSpec((tr, w), lambda i: (i, 3)),
            pl.BlockSpec((tr, w), lambda i: (i, 4)),
            pl.BlockSpec((tr, w), lambda i: (i, 5)),
            halo(4),
            halo(5),
            pl.BlockSpec((gpt, CONV_K - 1, w), lambda i: (i, 0, 0)),
            pl.BlockSpec((CONV_K, w), lambda i: (0, 0)),
            pl.BlockSpec((1, w), lambda i: (0, 0)),
            pl.BlockSpec((1, w), lambda i: (0, 0)),
            pl.BlockSpec((2 * w, d), lambda i: (0, 0)),
            pl.BlockSpec((tr, d), lambda i: (i, 0)),
            pl.BlockSpec((gpt, 1, d), lambda i: (i, 0, 0)),
        ],
        out_specs=[
            pl.BlockSpec((tr, d), lambda i: (i, 0)),
            pl.BlockSpec((gpt, SUBLANES, w), lambda i: (i, 0, 0)),
        ],
        out_shape=[
            jax.ShapeDtypeStruct((n, d), F32),
            jax.ShapeDtypeStruct((n // group, SUBLANES, w), F32),
        ],
        scratch_shapes=[pltpu.VMEM((tr, w), F32), pltpu.VMEM((tr, w), F32), pltpu.VMEM((tr, 2 * w), BF16)],
        name="mix_out",
        compiler_params=_params("parallel"),
    )(att_p, att_s, proj, proj, proj, proj, proj, state, w_conv, norm_att.reshape(1, w), norm_conv.reshape(1, w),
      w_out_bf16, x, g1)


def _staircase():
    return [(r, c) for r in range(PEER_TOPK) for c in range(PEER_TOPK) if (r + 1) * (c + 1) <= PEER_TOPK]


def _sort16_network():
    pairs, n, p = [], 16, 1
    while p < n:
        k = p
        while k >= 1:
            for j in range(k % p, n - k, 2 * k):
                for i in range(min(k, n - j - k)):
                    if (i + j) // (2 * p) == (i + j + k) // (2 * p):
                        pairs.append((i + j, i + j + k))
            k //= 2
        p *= 2
    return pairs


def _top16_values(slabs):
    x = list(slabs)
    for i, j in _sort16_network():
        x[i], x[j] = jnp.maximum(x[i], x[j]), jnp.minimum(x[i], x[j])
    for shift in (4, 2, 1):
        z = [jnp.maximum(x[r], pltpu.roll(x[PEER_TOPK - 1 - r], shift, axis=0)) for r in range(PEER_TOPK)]
        for dist in (8, 4, 2, 1):
            for i in range(PEER_TOPK):
                if i & dist == 0:
                    z[i], z[i + dist] = jnp.maximum(z[i], z[i + dist]), jnp.minimum(z[i], z[i + dist])
        x = z
    return x


def _top16_ranked(s, a_ref, slot):
    nk, tt = s.shape
    kidx = lax.broadcasted_iota(jnp.int32, (nk, tt), 0).astype(F32)
    work = s
    rank = jnp.full((nk, tt), float(PEER_TOPK), F32)
    for r in range(PEER_TOPK):
        m = jnp.max(work, axis=0, keepdims=True)
        first = jnp.min(jnp.where(work == m, kidx, float(nk)), axis=0, keepdims=True)
        hit = kidx == first
        rank = jnp.where(hit, float(r), rank)
        work = jnp.where(hit, -jnp.inf, work)
        a_ref[r, pl.ds(slot, 1), :] = m
    return rank


def _kept_per_row(a_ref, l_ref, heads):
    cands = _staircase()
    a0 = [a_ref[r, 0:heads, :] for r in range(PEER_TOPK)]
    a1 = [a_ref[c, heads:2 * heads, :] for c in range(PEER_TOPK)]
    sums = [a0[r] + a1[c] for r, c in cands]
    ahead = [0.0] * len(cands)
    for xi, (rx, cx) in enumerate(cands):
        for yi in range(xi):
            ry, cy = cands[yi]
            if ry <= rx and cy <= cx:
                ahead[xi] = ahead[xi] + 1.0
            else:
                y_first = sums[yi] >= sums[xi]
                ahead[xi] = ahead[xi] + jnp.where(y_first, 1.0, 0.0)
                ahead[yi] = ahead[yi] + jnp.where(y_first, 0.0, 1.0)
    zsum = jnp.zeros_like(sums[0])
    kept = [jnp.zeros_like(sums[0]) for _ in range(PEER_TOPK)]
    for xi, (rx, cx) in enumerate(cands):
        keep = jnp.where(ahead[xi] < float(PEER_TOPK), 1.0, 0.0)
        kept[rx] = kept[rx] + keep
        zsum = zsum + keep * jnp.exp(sums[xi] - sums[0])
    for r in range(PEER_TOPK):
        l_ref[r] = kept[r]
    l_ref[PEER_TOPK] = 1.0 / zsum


def _peer_select_kernel(q_ref, key_ref, ea_ref, lr_ref, eb_ref, cr_ref, s_ref, rk_ref, a_ref, l_ref,
                        *, heads):
    nk, tt = s_ref.shape[1], s_ref.shape[2]
    nslab = nk // SUBLANES
    ambiguous = jnp.zeros((1, tt), F32)
    for hp in range(2 * heads):
        slot = (hp % 2) * heads + hp // 2
        s = lax.dot_general(key_ref[hp], q_ref[:, hp * LANES:(hp + 1) * LANES], (((1,), (1,)), ((), ())),
                            precision=lax.Precision.HIGHEST, preferred_element_type=F32)
        s_ref[slot] = s
        slabs = [s[v * SUBLANES:(v + 1) * SUBLANES, :] for v in range(nslab)]
        best = _top16_values(slabs)
        for r in range(PEER_TOPK):
            a_ref[r, slot:slot + 1, :] = best[r][0:1, :]
        at_least_last = jnp.zeros((SUBLANES, tt), F32)
        for v in range(nslab):
            at_least_last = at_least_last + jnp.where(slabs[v] >= best[PEER_TOPK - 1], 1.0, 0.0)
        bad = jnp.sum(at_least_last, axis=0, keepdims=True) != float(PEER_TOPK)
        for r in range(PEER_TOPK - 1):
            bad = jnp.logical_or(bad, best[r][0:1, :] == best[r + 1][0:1, :])
        ambiguous = jnp.maximum(ambiguous, jnp.where(bad, 1.0, 0.0))
    redo = jnp.max(ambiguous) > 0.0

    @pl.when(redo)
    def _():
        def one_slot(slot, carry):
            rk_ref[slot] = _top16_ranked(s_ref[slot], a_ref, slot)
            return carry
        lax.fori_loop(0, 2 * heads, one_slot, 0)

    _kept_per_row(a_ref, l_ref, heads)

    for h in range(heads):
        ea_ref[h] = jnp.exp(s_ref[h] - a_ref[0, h:h + 1, :]) * l_ref[PEER_TOPK, h:h + 1, :]
        eb_ref[h] = jnp.exp(s_ref[heads + h] - a_ref[0, heads + h:heads + h + 1, :]).astype(eb_ref.dtype)

    @pl.when(jnp.logical_not(redo))
    def _():
        for h in range(heads):
            s0, s1 = s_ref[h], s_ref[heads + h]
            lrow = jnp.zeros((nk, tt), F32)
            crank = jnp.full((nk, tt), float(PEER_TOPK), F32)
            for r in range(PEER_TOPK):
                lrow = jnp.where(s0 == a_ref[r, h:h + 1, :], l_ref[r, h:h + 1, :], lrow)
                crank = jnp.where(s1 == a_ref[r, heads + h:heads + h + 1, :], float(r), crank)
            lr_ref[h] = lrow
            cr_ref[h] = crank.astype(cr_ref.dtype)

    @pl.when(redo)
    def _():
        for h in range(heads):
            rk0 = rk_ref[h]
            lrow = jnp.zeros((nk, tt), F32)
            for r in range(PEER_TOPK):
                lrow = jnp.where(rk0 == float(r), l_ref[r, h:h + 1, :], lrow)
            lr_ref[h] = lrow
            cr_ref[h] = rk_ref[heads + h].astype(cr_ref.dtype)


def _peer_select(q, keys):
    n, dq = q.shape
    heads, _, nk, dk = keys.shape
    assert nk == PEER_TOPK * SUBLANES and dk == LANES
    tt = _pick(n, (256, 128))
    out = jax.ShapeDtypeStruct((heads, nk, n), F32)
    out_j = jax.ShapeDtypeStruct((heads, nk, n), BF16)
    ospec = pl.BlockSpec((heads, nk, tt), lambda t: (0, 0, t))
    return pl.pallas_call(
        functools.partial(_peer_select_kernel, heads=heads),
        grid=(n // tt,),
        in_specs=[
            pl.BlockSpec((tt, dq), lambda t: (t, 0)),
            pl.BlockSpec((2 * heads, nk, dk), lambda t: (0, 0, 0)),
        ],
        out_specs=[ospec, ospec, ospec, ospec],
        out_shape=[out, out, out_j, out_j],
        scratch_shapes=[
            pltpu.VMEM((2 * heads, nk, tt), F32),
            pltpu.VMEM((2 * heads, nk, tt), F32),
            pltpu.VMEM((PEER_TOPK, 2 * heads, tt), F32),
            pltpu.VMEM((PEER_TOPK + 1, heads, tt), F32),
        ],
        name="peer_select",
        compiler_params=_params("parallel"),
    )(q, keys.reshape(2 * heads, nk, dk))


PEER_CHUNK = 1024


def _peer_dense_kernel(x_ref, g2_ref, h_ref, u_ref, vt_ref, vt_last_ref, ea_ref, lr_ref, eb_ref, cr_ref, o_ref,
                       acc_ref, act_ref, coef_lo_ref, coef_hi_ref, *, heads, group):
    tt = act_ref.shape[1]
    c = pl.program_id(1)
    one_trip = jnp.minimum(c, 0) + 1
    iph = PEER_CHUNK // 2 // LANES

    @pl.when(jnp.logical_and(pl.program_id(0) == 0, c == 0))
    def _():
        coef_hi_ref[...] = jnp.zeros_like(coef_hi_ref)

    @pl.when(c == 0)
    def _():
        acc_ref[...] = jnp.zeros_like(acc_ref)

    hrows = PEER_CHUNK // 2

    def first_matmul(half):
        rows = slice(half * hrows, (half + 1) * hrows)
        act_ref[rows, :] = lax.dot_general(u_ref[rows, :], h_ref[...], (((1,), (1,)), ((), ())),
                                           preferred_element_type=F32)

    first_matmul(0)

    def build(half, dst):
        tb = min(tt, 2 * LANES)
        for s in range(iph):
            il = half * iph + s
            rows = slice(il * LANES, (il + 1) * LANES)
            for t0 in range(0, tt, tb):
                cols = slice(t0, t0 + tb)
                wsum = jnp.zeros((LANES, tb), BF16)
                for h in range(heads):
                    lrow = lr_ref[h, il:il + 1, cols].astype(BF16)
                    ea = ea_ref[h, il:il + 1, cols].astype(BF16)
                    wsum = wsum + jnp.where(cr_ref[h, :, cols] < lrow, eb_ref[h, :, cols] * ea,
                                            jnp.zeros((), BF16))
                g = jax.nn.gelu(act_ref[rows, cols], approximate=True)
                dst[s * LANES:(s + 1) * LANES, cols] = (wsum.astype(F32) * g).astype(BF16)

    def region(half, src, dst):
        def body(_, carry):
            build(half, dst)
            acc_ref[...] += jnp.dot(vt_ref[half], src[...], preferred_element_type=F32)
            if half == 0:
                first_matmul(1)
            return carry
        lax.fori_loop(0, one_trip, body, 0)

    region(0, coef_hi_ref, coef_lo_ref)
    region(1, coef_lo_ref, coef_hi_ref)

    @pl.when(c == pl.num_programs(1) - 1)
    def _():
        acc_ref[...] += jnp.dot(vt_last_ref[0], coef_hi_ref[...], preferred_element_type=F32)
        y = acc_ref[...].T
        for g in range(tt // group):
            rs = slice(g * group, (g + 1) * group)
            o_ref[rs, :] = x_ref[rs, :] + g2_ref[g] * y[rs]


def _peer_dense(x, g2, h_bf16, u_bf16, v_bf16, ea, lr, eb, cr, *, group):
    n, d = h_bf16.shape
    e = u_bf16.shape[0]
    heads, nk, _ = ea.shape
    assert e % PEER_CHUNK == 0
    tt = _pick(n, (512, 256, 128))
    half = PEER_CHUNK // 2
    nhalves = e // half
    vt = v_bf16.reshape(nhalves, half, d).transpose(0, 2, 1)
    vt = jnp.concatenate([jnp.zeros((1, d, half), BF16), vt], axis=0)
    sel_i = pl.BlockSpec((heads, PEER_CHUNK // LANES, tt), lambda t, c: (0, c, t))
    sel_j = pl.BlockSpec((heads, nk, tt), lambda t, c: (0, 0, t))
    return pl.pallas_call(
        functools.partial(_peer_dense_kernel, heads=heads, group=group),
        grid=(n // tt, e // PEER_CHUNK),
        in_specs=[
            pl.BlockSpec((tt, d), lambda t, c: (t, 0)),
            pl.BlockSpec((tt // group, 1, d), lambda t, c: (t, 0, 0)),
            pl.BlockSpec((tt, d), lambda t, c: (t, 0)),
            pl.BlockSpec((PEER_CHUNK, d), lambda t, c: (c, 0)),
            pl.BlockSpec((2, d, half), lambda t, c: (c, 0, 0)),
            pl.BlockSpec((1, d, half), lambda t, c: (nhalves, 0, 0)),
            sel_i, sel_i, sel_j, sel_j,
        ],
        out_specs=pl.BlockSpec((tt, d), lambda t, c: (t, 0)),
        out_shape=jax.ShapeDtypeStruct((n, d), F32),
        scratch_shapes=[pltpu.VMEM((d, tt), F32), pltpu.VMEM((PEER_CHUNK, tt), F32),
                        pltpu.VMEM((half, tt), BF16), pltpu.VMEM((half, tt), BF16)],
        name="peer_dense",
        compiler_params=_params("arbitrary", "arbitrary"),
    )(x, g2, h_bf16, u_bf16, vt, vt, ea, lr, eb, cr)


def _final_norm_kernel(x_ref, g_ref, o_ref):
    x = x_ref[...]
    o_ref[...] = x * lax.rsqrt(jnp.mean(x * x, axis=-1, keepdims=True) + EPS) * g_ref[...]


def _final_norm(x, gain, row0, rows):
    d = x.shape[1]
    tr = _pick(math.gcd(row0, rows), (512, 256, 128, 64))
    first = row0 // tr
    return pl.pallas_call(
        _final_norm_kernel,
        grid=(rows // tr,),
        in_specs=[pl.BlockSpec((tr, d), lambda i: (first + i, 0)), pl.BlockSpec((1, d), lambda i: (0, 0))],
        out_specs=pl.BlockSpec((tr, d), lambda i: (i, 0)),
        out_shape=jax.ShapeDtypeStruct((rows, d), F32),
        name="final_norm",
        compiler_params=_params("parallel"),
    )(x, gain.reshape(1, d))


def kernel(x_prompt, x_sample, cache_k, cache_v, state_conv, c_prompt, c_sample, w_ada, b_ada, norm_mix,
           w_in, w_conv, norm_att_out, norm_conv_out, w_out, norm_ffn, w_peer_q, peer_keys, peer_u, peer_v,
           norm_final):
    bp, tp, d = x_prompt.shape
    bs, ts, _ = x_sample.shape
    depth = w_in.shape[0]
    past = cache_k.shape[2]
    heads = cache_k.shape[3]
    w = heads * LANES
    assert cache_k.shape[4] == LANES and state_conv.shape[-1] == w and w_in.shape[2] == 6 * w
    assert tp % ts == 0 and ts % SUBLANES == 0
    group = ts
    n_p, n_s = bp * tp, bs * ts
    n = n_p + n_s
    scale = float(LANES) ** -0.5

    x = jnp.concatenate([x_prompt.reshape(n_p, d), x_sample.reshape(n_s, d)], axis=0)

    c_all = jnp.concatenate([c_prompt, c_sample], axis=0)
    r = c_all.shape[0]
    r_pad = -(-r // SUBLANES) * SUBLANES
    mod = _ada(jnp.pad(c_all, ((0, r_pad - r), (0, 0))), w_ada, b_ada)
    group_row = jnp.concatenate([jnp.repeat(jnp.arange(bp), tp // group), bp + jnp.arange(bs)])
    mod = jnp.take(mod, group_row, axis=1).reshape(depth, n // group, 6, 1, d)

    state =jnp.concatenate([jnp.zeros((depth, n_p // group, CONV_K - 1, w), F32), state_conv], axis=1)
    seq_end_group = jnp.concatenate([(jnp.arange(bp) + 1) * (tp // group) - 1, n_p // group + jnp.arange(bs)])

    kp = jnp.zeros((depth, n_p, heads, LANES), F32)
    vp = jnp.zeros((depth, n_p, heads, LANES), F32)
    ks = jnp.zeros((depth, n_s, heads, LANES), F32)
    vs = jnp.zeros((depth, n_s, heads, LANES), F32)
    cp, cs = [], []
    for l in range(depth):
        sh1, sc1, g1, sh2, sc2, g2 = (mod[l, :, k] for k in range(6))
        proj = _norm_mod_matmul(x, norm_mix[l], sc1, sh1, w_in[l].astype(BF16), group=group, emit_h=False)
        att_p, kp, vp = _attn_prompt(proj, kp, vp, l, batch=bp, seq=tp, heads=heads, scale=scale)
        att_s, ks, vs = _attn_sample_cached(proj, cache_k, cache_v, l, ks, vs, n_prompt=n_p, seq=ts, scale=scale)
        x, tails = _mix_out(att_p, att_s, proj, state[l], w_conv[l], norm_att_out[l], norm_conv_out[l],
                            w_out[l].astype(BF16), x, g1, group=group, n_prompt=n_p, seq=tp)
        q, h2 = _norm_mod_matmul(x, norm_ffn[l], sc2, sh2, w_peer_q[l].astype(BF16), group=group, emit_h=True)
        ea, lr, eb, cr = _peer_select(q, peer_keys[l])
        x = _peer_dense(x, g2, h2, peer_u[l].astype(BF16), peer_v[l].astype(BF16), ea, lr, eb, cr, group=group)

        conv_new = jnp.take(tails, seq_end_group, axis=0)[:, SUBLANES - (CONV_K - 1):, :]
        cp.append(conv_new[:bp])
        cs.append(conv_new[bp:])

    y_p = _final_norm(x, norm_final, 0, n_p)
    y_s = _final_norm(x, norm_final, n_p, n_s)
    return (y_p.reshape(bp, tp, d), y_s.reshape(bs, ts, d),
            kp.reshape(depth, bp, tp, heads, LANES), vp.reshape(depth, bp, tp, heads, LANES), jnp.stack(cp),
            ks.reshape(depth, bs, ts, heads, LANES), vs.reshape(depth, bs, ts, heads, LANES), jnp.stack(cs))
```

```python
import functools
import math

import jax
import jax.numpy as jnp
from jax import lax
from jax.experimental import pallas as pl
from jax.experimental.pallas import tpu as pltpu

F32 = jnp.float32
BF16 = jnp.bfloat16
EPS = 1e-6
LANES = 128
SUBLANES = 8
CONV_K = 3
PEER_TOPK = 16
VMEM_LIMIT_BYTES = 56 * 1024 * 1024
EXP_UNDERFLOW = -104.0


def _params(*sem):
    return pltpu.CompilerParams(dimension_semantics=sem, vmem_limit_bytes=VMEM_LIMIT_BYTES)


def _pick(n, cands):
    for c in cands:
        if n % c == 0:
            return c
    raise ValueError(f"no tile in {cands} divides {n}")


def _ada_kernel(c_ref, w_ref, b_ref, o_ref):
    c = c_ref[...]
    s = (c * jax.nn.sigmoid(c)).astype(BF16)
    o_ref[...] = jnp.dot(s, w_ref[...].astype(BF16), preferred_element_type=F32) + b_ref[...]


def _ada(c_all, w_ada, b_ada):
    depth, d, n6 = w_ada.shape
    r = c_all.shape[0]
    tn = _pick(n6, (1024, 512, 256, 128))
    return pl.pallas_call(
        _ada_kernel,
        grid=(depth, n6 // tn),
        in_specs=[
            pl.BlockSpec((r, d), lambda l, j: (0, 0)),
            pl.BlockSpec((None, d, tn), lambda l, j: (l, 0, j)),
            pl.BlockSpec((None, 1, tn), lambda l, j: (l, 0, j)),
        ],
        out_specs=pl.BlockSpec((None, r, tn), lambda l, j: (l, 0, j)),
        out_shape=jax.ShapeDtypeStruct((depth, r, n6), F32),
        name="ada",
        compiler_params=_params("parallel", "parallel"),
    )(c_all, w_ada, b_ada.reshape(depth, 1, n6))


def _nmm_kernel(x_ref, g_ref, sc_ref, sh_ref, w_ref, o_ref, *rest, group, emit_h):
    hs_ref = rest[-1]
    tr = x_ref.shape[0]

    @pl.when(pl.program_id(1) == 0)
    def _():
        x = x_ref[...]
        y = x * lax.rsqrt(jnp.mean(x * x, axis=-1, keepdims=True) + EPS) * g_ref[...]
        for g in range(tr // group):
            rows = slice(g * group, (g + 1) * group)
            hs_ref[rows, :] = (y[rows] * (1.0 + sc_ref[g]) + sh_ref[g]).astype(BF16)
        if emit_h:
            rest[0][...] = hs_ref[...]

    o_ref[...] = jnp.dot(hs_ref[...], w_ref[...], preferred_element_type=F32)


def _norm_mod_matmul(x, gain, sc, sh, w_bf16, *, group, emit_h):
    n, d = x.shape
    nout = w_bf16.shape[1]
    tr = _pick(n, (512, 256, 128, 64))
    tn = _pick(nout, (2048, 1024, 512, 256, 128))
    gpt = tr // group
    out_shape = [jax.ShapeDtypeStruct((n, nout), F32)]
    out_specs = [pl.BlockSpec((tr, tn), lambda i, j: (i, j))]
    if emit_h:
        out_shape.append(jax.ShapeDtypeStruct((n, d), BF16))
        out_specs.append(pl.BlockSpec((tr, d), lambda i, j: (i, 0)))
    res = pl.pallas_call(
        functools.partial(_nmm_kernel, group=group, emit_h=emit_h),
        grid=(n // tr, nout // tn),
        in_specs=[
            pl.BlockSpec((tr, d), lambda i, j: (i, 0)),
            pl.BlockSpec((1, d), lambda i, j: (0, 0)),
            pl.BlockSpec((gpt, 1, d), lambda i, j: (i, 0, 0)),
            pl.BlockSpec((gpt, 1, d), lambda i, j: (i, 0, 0)),
            pl.BlockSpec((d, tn), lambda i, j: (0, j)),
        ],
        out_specs=out_specs,
        out_shape=out_shape,
        scratch_shapes=[pltpu.VMEM((tr, d), BF16)],
        name="norm_mod_matmul",
        compiler_params=_params("parallel", "arbitrary"),
    )(x, gain.reshape(1, d), sc, sh, w_bf16)
    return res if emit_h else res[0]


def _strict_upper(n):
    return (lax.broadcasted_iota(jnp.int32, (n, n), 0) > lax.broadcasted_iota(jnp.int32, (n, n), 1)).astype(BF16)


def _sb_blocks(qs, ks, vs, tri, carries, accs, *, masked, scale):
    n = len(qs)
    tq = qs[0].shape[0]
    zs = [lax.dot_general(qs[h], ks[h], (((1,), (1,)), ((), ())), preferred_element_type=F32) * scale
          for h in range(n)]
    ls = [jnp.minimum(z, 0.0) - jnp.log1p(jnp.exp(-jnp.abs(z))) for z in zs]
    lr = [l - z for l, z in zip(ls, zs)]
    if masked:
        shape = zs[0].shape
        vis = lax.broadcasted_iota(jnp.int32, shape, 1) < lax.broadcasted_iota(jnp.int32, shape, 0)
        lr = [jnp.where(vis, x, 0.0) for x in lr]
    lr_all = jnp.concatenate(lr, axis=0)
    hi = lr_all.astype(BF16)
    r1 = lr_all - hi.astype(F32)
    mid = r1.astype(BF16)
    lo = (r1 - mid.astype(F32)).astype(BF16)
    suf = (jnp.dot(hi, tri, preferred_element_type=F32) + jnp.dot(mid, tri, preferred_element_type=F32)
           + jnp.dot(lo, tri, preferred_element_type=F32))
    ws = []
    for h in range(n):
        w = jnp.exp(ls[h] + suf[h * tq:(h + 1) * tq] + carries[h])
        if masked:
            w = jnp.where(vis, w, 0.0)
        ws.append(w.astype(BF16))
    accs = [accs[h] + jnp.dot(ws[h], vs[h], preferred_element_type=F32) for h in range(n)]
    carries = [carries[h] + jnp.sum(lr[h], axis=1, keepdims=True) for h in range(n)]
    return carries, accs


def _attn_prompt_kernel(q_ref, k_ref, v_ref, *rest, heads, scale):
    o_ref, ko_ref, vo_ref = rest[-3:]
    tq = q_ref.shape[0]
    qi = pl.program_id(1)
    tri = _strict_upper(tq)
    cols = [slice(h * LANES, (h + 1) * LANES) for h in range(heads)]
    qs = [q_ref[:, cols[h]].astype(BF16) for h in range(heads)]
    own = pl.ds(pl.multiple_of(qi * tq, tq), tq)
    for h in range(heads):
        ko_ref[:, h, :] = k_ref[own, cols[h]]
        vo_ref[:, h, :] = v_ref[own, cols[h]]

    def all_heads(kb, carries, accs, masked):
        rows = pl.ds(pl.multiple_of(kb * tq, tq), tq)
        return _sb_blocks(qs, [k_ref[rows, cols[h]].astype(BF16) for h in range(heads)],
                          [v_ref[rows, cols[h]].astype(BF16) for h in range(heads)], tri, carries, accs,
                          masked=masked, scale=scale)

    carries, accs = all_heads(qi, [jnp.zeros((tq, 1), F32)] * heads, [jnp.zeros((tq, LANES), F32)] * heads, True)

    def cond(st):
        kb, carries, _ = st
        live = functools.reduce(jnp.maximum, carries)
        return jnp.logical_and(kb >= 0, jnp.max(live) > EXP_UNDERFLOW)

    def body(st):
        kb, carries, accs = st
        carries, accs = all_heads(kb, carries, accs, False)
        return kb - 1, carries, accs

    _, _, accs = lax.while_loop(cond, body, (qi - 1, carries, accs))
    for h in range(heads):
        o_ref[:, cols[h]] = accs[h]


def _attn_prompt(proj, k_all, v_all, layer, *, batch, seq, heads, scale):
    w = heads * LANES
    tq = _pick(seq, (128, 64))
    nq = seq // tq
    kv_spec = pl.BlockSpec((None, tq, heads, LANES), lambda b, i: (layer, b * nq + i, 0, 0))
    return pl.pallas_call(
        functools.partial(_attn_prompt_kernel, heads=heads, scale=scale),
        grid=(batch, nq),
        in_specs=[
            pl.BlockSpec((tq, w), lambda b, i: (b * nq + i, 0)),
            pl.BlockSpec((seq, w), lambda b, i: (b, 1)),
            pl.BlockSpec((seq, w), lambda b, i: (b, 2)),
            pl.BlockSpec(memory_space=pl.ANY),
            pl.BlockSpec(memory_space=pl.ANY),
        ],
        out_specs=[pl.BlockSpec((tq, w), lambda b, i: (b * nq + i, 0)), kv_spec, kv_spec],
        out_shape=[jax.ShapeDtypeStruct((batch * seq, w), F32),
                   jax.ShapeDtypeStruct(k_all.shape, F32), jax.ShapeDtypeStruct(v_all.shape, F32)],
        input_output_aliases={3: 1, 4: 2},
        name="attn_prompt",
        compiler_params=_params("parallel", "parallel"),
    )(proj, proj, proj, k_all, v_all)


def _attn_sample_kernel(q_ref, kn_ref, vn_ref, kc_ref, vc_ref, *rest, heads, tk, scale, emit_kv):
    if emit_kv:
        _, _, o_ref, cmax_ref, ko_ref, vo_ref, carry_ref, acc_ref = rest
    else:
        o_ref, cmax_ref, carry_ref, acc_ref = rest
    tq = q_ref.shape[0]
    tkp = kc_ref.shape[0]
    j = pl.program_id(1)

    def hs(h):
        return slice(h * LANES, (h + 1) * LANES)

    def qs():
        return [q_ref[:, hs(h)].astype(BF16) for h in range(heads)]

    def put(carries, accs):
        for h in range(heads):
            carry_ref[h] = jnp.broadcast_to(carries[h], (tq, LANES))
            acc_ref[:, hs(h)] = accs[h]

    @pl.when(j == 0)
    def _():
        put(*_sb_blocks(qs(), [kn_ref[:, hs(h)].astype(BF16) for h in range(heads)],
                        [vn_ref[:, hs(h)].astype(BF16) for h in range(heads)], _strict_upper(tq),
                        [jnp.zeros((tq, 1), F32)] * heads, [jnp.zeros((tq, LANES), F32)] * heads,
                        masked=True, scale=scale))
        if emit_kv:
            for h in range(heads):
                ko_ref[:, h, :] = kn_ref[:, hs(h)]
                vo_ref[:, h, :] = vn_ref[:, hs(h)]

    @pl.when(j > 0)
    def _():
        tri = _strict_upper(tk)
        for sb in reversed(range(tkp // tk)):
            live = functools.reduce(jnp.maximum, [carry_ref[h] for h in range(heads)])

            @pl.when(jnp.max(live) > EXP_UNDERFLOW)
            def _():
                rows = slice(sb * tk, (sb + 1) * tk)
                put(*_sb_blocks(qs(), [kc_ref[rows, h, :].astype(BF16) for h in range(heads)],
                                [vc_ref[rows, h, :].astype(BF16) for h in range(heads)], tri,
                                [carry_ref[h][:, 0:1] for h in range(heads)],
                                [acc_ref[:, hs(h)] for h in range(heads)], masked=False, scale=scale))

    @pl.when(j == pl.num_programs(1) - 1)
    def _():
        o_ref[...] = acc_ref[...]
        for h in range(heads):
            cmax_ref[h:h + 1, :] = jnp.max(carry_ref[h], axis=0, keepdims=True)


SAMPLE_CACHE_BLOCK = 512


def _attn_sample(proj, cache_k, cache_v, layer, kv_out, *, n_blocks, n_prompt, seq, scale):
    _, batch, past, heads, _ = cache_k.shape
    w = heads * LANES
    tkp = _pick(past, (SAMPLE_CACHE_BLOCK, 256, 128))
    tk = min(tkp, 256)
    last_block = past // tkp - 1
    row0 = n_prompt // seq
    emit_kv = kv_out is not None
    cache_spec = pl.BlockSpec((None, None, tkp, heads, LANES),
                              lambda b, j: (layer, b, last_block + 1 - jnp.maximum(j, 1), 0, 0))
    in_specs = [
        pl.BlockSpec((seq, w), lambda b, j: (row0 + b, 0)),
        pl.BlockSpec((seq, w), lambda b, j: (row0 + b, 1)),
        pl.BlockSpec((seq, w), lambda b, j: (row0 + b, 2)),
        cache_spec,
        cache_spec,
    ]
    out_specs = [
        pl.BlockSpec((seq, w), lambda b, j: (b, 0)),
        pl.BlockSpec((None, heads, LANES), lambda b, j: (b, 0, 0)),
    ]
    out_shape = [
        jax.ShapeDtypeStruct((batch * seq, w), F32),
        jax.ShapeDtypeStruct((batch, heads, LANES), F32),
    ]
    args = [proj, proj, proj, cache_k, cache_v]
    aliases = {}
    if emit_kv:
        kv_spec = pl.BlockSpec((None, seq, heads, LANES), lambda b, j: (layer, b, 0, 0))
        in_specs += [pl.BlockSpec(memory_space=pl.ANY)] * 2
        out_specs += [kv_spec, kv_spec]
        out_shape += [jax.ShapeDtypeStruct(a.shape, F32) for a in kv_out]
        args += list(kv_out)
        aliases = {5: 2, 6: 3}
    return pl.pallas_call(
        functools.partial(_attn_sample_kernel, heads=heads, tk=tk, scale=scale, emit_kv=emit_kv),
        grid=(batch, 1 + n_blocks),
        in_specs=in_specs,
        out_specs=out_specs,
        out_shape=out_shape,
        input_output_aliases=aliases,
        scratch_shapes=[pltpu.VMEM((heads, seq, LANES), F32), pltpu.VMEM((seq, w), F32)],
        name="attn_sample",
        compiler_params=_params("parallel", "arbitrary"),
    )(*args)


def _attn_sample_cached(proj, cache_k, cache_v, layer, k_all, v_all, *, n_prompt, seq, scale):
    past = cache_k.shape[2]
    kw = dict(n_prompt=n_prompt, seq=seq, scale=scale)
    n_all = past // _pick(past, (SAMPLE_CACHE_BLOCK, 256, 128))
    att, cmax, k_all, v_all = _attn_sample(proj, cache_k, cache_v, layer, (k_all, v_all), n_blocks=1, **kw)
    if n_all > 1:
        att = lax.cond(
            jnp.max(cmax) > EXP_UNDERFLOW,
            lambda ck, cv: _attn_sample(proj, ck, cv, layer, None, n_blocks=n_all, **kw)[0],
            lambda ck, cv: att,
            cache_k, cache_v)
    return att, k_all, v_all


def _mix_kernel(att_p_ref, att_s_ref, bg_ref, cg_ref, xc_ref, cgh_ref, xch_ref, st_ref, wc_ref, na_ref, nc_ref,
                wo_ref, x_ref, g1_ref, o_ref, tail_ref, um1_ref, um2_ref, mix_ref,
                *, group, prompt_groups, groups_per_seq):
    tr, w = bg_ref.shape
    gpt = tr // group
    i = pl.program_id(0)

    u = cg_ref[...] * xc_ref[...]
    uh = cgh_ref[...] * xch_ref[...]
    rows = lax.broadcasted_iota(jnp.int32, (tr, w), 0)
    h7 = uh[SUBLANES - 1:SUBLANES, :]
    h6 = uh[SUBLANES - 2:SUBLANES - 1, :]
    um1_ref[...] = jnp.where(rows == 0, h7, pltpu.roll(u, 1, axis=0))
    um2_ref[...] = jnp.where(rows == 0, h6, jnp.where(rows == 1, h7, pltpu.roll(u, 2, axis=0)))
    r8 = lax.broadcasted_iota(jnp.int32, (SUBLANES, w), 0)
    for g in range(gpt):
        gg = i * gpt + g
        starts_sequence = jnp.logical_or(gg >= prompt_groups, gg % groups_per_seq == 0)
        head = slice(g * group, g * group + SUBLANES)

        @pl.when(starts_sequence)
        def _():
            st0 = st_ref[g, 0:1, :]
            st1 = st_ref[g, 1:2, :]
            um1_ref[head, :] = jnp.where(r8 == 0, st1, um1_ref[head, :])
            um2_ref[head, :] = jnp.where(r8 == 0, st0, jnp.where(r8 == 1, st1, um2_ref[head, :]))

        tail_ref[g] = u[(g + 1) * group - SUBLANES:(g + 1) * group, :]

    conv = wc_ref[0:1, :] * um2_ref[...] + wc_ref[1:2, :] * um1_ref[...] + wc_ref[2:3, :] * u
    cv = bg_ref[...] * conv
    att = jnp.where(i * gpt < prompt_groups, att_p_ref[...], att_s_ref[...])
    for k in range(w // LANES):
        cs = slice(k * LANES, (k + 1) * LANES)
        a = att[:, cs]
        mix_ref[:, cs] = (a * lax.rsqrt(jnp.mean(a * a, axis=-1, keepdims=True) + EPS)
                          * na_ref[:, cs]).astype(BF16)
        c = cv[:, cs]
        mix_ref[:, w + k * LANES:w + (k + 1) * LANES] = (
            c * lax.rsqrt(jnp.mean(c * c, axis=-1, keepdims=True) + EPS) * nc_ref[:, cs]).astype(BF16)
    y = jnp.dot(mix_ref[...], wo_ref[...], preferred_element_type=F32)
    for g in range(gpt):
        rs = slice(g * group, (g + 1) * group)
        o_ref[rs, :] = x_ref[rs, :] + g1_ref[g] * y[rs]


def _mix_out(att_p, att_s, proj, state, w_conv, norm_att, norm_conv, w_out_bf16, x, g1, *, group, n_prompt, seq):
    n, d = x.shape
    w = att_p.shape[1]
    tr = _pick(math.gcd(n_prompt, n - n_prompt), (256, 128, 64))
    gpt = tr // group
    hb = tr // SUBLANES
    ptiles = n_prompt // tr
    kern = functools.partial(_mix_kernel, group=group, prompt_groups=n_prompt // group,
                             groups_per_seq=seq // group)
    halo = lambda col: pl.BlockSpec((SUBLANES, w), lambda i: (jnp.maximum(i * hb - 1, 0), col))
    return pl.pallas_call(
        kern,
        grid=(n // tr,),
        in_specs=[
            pl.BlockSpec((tr, w), lambda i: (jnp.minimum(i, ptiles - 1), 0)),
            pl.BlockSpec((tr, w), lambda i: (jnp.maximum(i - ptiles, 0), 0)),
            pl.BlockSpec((tr, w), lambda i: (i, 3)),
            pl.BlockSpec((tr, w), lambda i: (i, 4)),
            pl.BlockSpec((tr, w), lambda i: (i, 5)),
            halo(4),
            halo(5),
            pl.BlockSpec((gpt, CONV_K - 1, w), lambda i: (i, 0, 0)),
            pl.BlockSpec((CONV_K, w), lambda i: (0, 0)),
            pl.BlockSpec((1, w), lambda i: (0, 0)),
            pl.BlockSpec((1, w), lambda i: (0, 0)),
            pl.BlockSpec((2 * w, d), lambda i: (0, 0)),
            pl.BlockSpec((tr, d), lambda i: (i, 0)),
            pl.BlockSpec((gpt, 1, d), lambda i: (i, 0, 0)),
        ],
        out_specs=[
            pl.BlockSpec((tr, d), lambda i: (i, 0)),
            pl.BlockSpec((gpt, SUBLANES, w), lambda i: (i, 0, 0)),
        ],
        out_shape=[
            jax.ShapeDtypeStruct((n, d), F32),
            jax.ShapeDtypeStruct((n // group, SUBLANES, w), F32),
        ],
        scratch_shapes=[pltpu.VMEM((tr, w), F32), pltpu.VMEM((tr, w), F32), pltpu.VMEM((tr, 2 * w), BF16)],
        name="mix_out",
        compiler_params=_params("parallel"),
    )(att_p, att_s, proj, proj, proj, proj, proj, state, w_conv, norm_att.reshape(1, w), norm_conv.reshape(1, w),
      w_out_bf16, x, g1)


def _staircase():
    return [(r, c) for r in range(PEER_TOPK) for c in range(PEER_TOPK) if (r + 1) * (c + 1) <= PEER_TOPK]


def _sort16_network():
    pairs, n, p = [], 16, 1
    while p < n:
        k = p
        while k >= 1:
            for j in range(k % p, n - k, 2 * k):
                for i in range(min(k, n - j - k)):
                    if (i + j) // (2 * p) == (i + j + k) // (2 * p):
                        pairs.append((i + j, i + j + k))
            k //= 2
        p *= 2
    return pairs


def _top16_values(slabs):
    x = list(slabs)
    for i, j in _sort16_network():
        x[i], x[j] = jnp.maximum(x[i], x[j]), jnp.minimum(x[i], x[j])
    for shift in (4, 2, 1):
        z = [jnp.maximum(x[r], pltpu.roll(x[PEER_TOPK - 1 - r], shift, axis=0)) for r in range(PEER_TOPK)]
        for dist in (8, 4, 2, 1):
            for i in range(PEER_TOPK):
                if i & dist == 0:
                    z[i], z[i + dist] = jnp.maximum(z[i], z[i + dist]), jnp.minimum(z[i], z[i + dist])
        x = z
    return x


def _top16_ranked(s, a_ref, slot):
    nk, tt = s.shape
    kidx = lax.broadcasted_iota(jnp.int32, (nk, tt), 0).astype(F32)
    work = s
    rank = jnp.full((nk, tt), float(PEER_TOPK), F32)
    for r in range(PEER_TOPK):
        m = jnp.max(work, axis=0, keepdims=True)
        first = jnp.min(jnp.where(work == m, kidx, float(nk)), axis=0, keepdims=True)
        hit = kidx == first
        rank = jnp.where(hit, float(r), rank)
        work = jnp.where(hit, -jnp.inf, work)
        a_ref[r, pl.ds(slot, 1), :] = m
    return rank


def _kept_per_row(a_ref, l_ref, heads):
    cands = _staircase()
    a0 = [a_ref[r, 0:heads, :] for r in range(PEER_TOPK)]
    a1 = [a_ref[c, heads:2 * heads, :] for c in range(PEER_TOPK)]
    sums = [a0[r] + a1[c] for r, c in cands]
    ahead = [0.0] * len(cands)
    for xi, (rx, cx) in enumerate(cands):
        for yi in range(xi):
            ry, cy = cands[yi]
            if ry <= rx and cy <= cx:
                ahead[xi] = ahead[xi] + 1.0
            else:
                y_first = sums[yi] >= sums[xi]
                ahead[xi] = ahead[xi] + jnp.where(y_first, 1.0, 0.0)
                ahead[yi] = ahead[yi] + jnp.where(y_first, 0.0, 1.0)
    zsum = jnp.zeros_like(sums[0])
    kept = [jnp.zeros_like(sums[0]) for _ in range(PEER_TOPK)]
    for xi, (rx, cx) in enumerate(cands):
        keep = jnp.where(ahead[xi] < float(PEER_TOPK), 1.0, 0.0)
        kept[rx] = kept[rx] + keep
        zsum = zsum + keep * jnp.exp(sums[xi] - sums[0])
    for r in range(PEER_TOPK):
        l_ref[r] = kept[r]
    l_ref[PEER_TOPK] = 1.0 / zsum


def _peer_select_kernel(q_ref, key_ref, ea_ref, lr_ref, eb_ref, cr_ref, s_ref, rk_ref, a_ref, l_ref,
                        *, heads):
    nk, tt = s_ref.shape[1], s_ref.shape[2]
    nslab = nk // SUBLANES
    ambiguous = jnp.zeros((1, tt), F32)
    for hp in range(2 * heads):
        slot = (hp % 2) * heads + hp // 2
        s = lax.dot_general(key_ref[hp], q_ref[:, hp * LANES:(hp + 1) * LANES], (((1,), (1,)), ((), ())),
                            precision=lax.Precision.HIGHEST, preferred_element_type=F32)
        s_ref[slot] = s
        slabs = [s[v * SUBLANES:(v + 1) * SUBLANES, :] for v in range(nslab)]
        best = _top16_values(slabs)
        for r in range(PEER_TOPK):
            a_ref[r, slot:slot + 1, :] = best[r][0:1, :]
        at_least_last = jnp.zeros((SUBLANES, tt), F32)
        for v in range(nslab):
            at_least_last = at_least_last + jnp.where(slabs[v] >= best[PEER_TOPK - 1], 1.0, 0.0)
        bad = jnp.sum(at_least_last, axis=0, keepdims=True) != float(PEER_TOPK)
        for r in range(PEER_TOPK - 1):
            bad = jnp.logical_or(bad, best[r][0:1, :] == best[r + 1][0:1, :])
        ambiguous = jnp.maximum(ambiguous, jnp.where(bad, 1.0, 0.0))
    redo = jnp.max(ambiguous) > 0.0

    @pl.when(redo)
    def _():
        def one_slot(slot, carry):
            rk_ref[slot] = _top16_ranked(s_ref[slot], a_ref, slot)
            return carry
        lax.fori_loop(0, 2 * heads, one_slot, 0)

    _kept_per_row(a_ref, l_ref, heads)

    for h in range(heads):
        ea_ref[h] = jnp.exp(s_ref[h] - a_ref[0, h:h + 1, :]) * l_ref[PEER_TOPK, h:h + 1, :]
        eb_ref[h] = jnp.exp(s_ref[heads + h] - a_ref[0, heads + h:heads + h + 1, :]).astype(eb_ref.dtype)

    @pl.when(jnp.logical_not(redo))
    def _():
        for h in range(heads):
            s0, s1 = s_ref[h], s_ref[heads + h]
            lrow = jnp.zeros((nk, tt), F32)
            crank = jnp.full((nk, tt), float(PEER_TOPK), F32)
            for r in range(PEER_TOPK):
                lrow = jnp.where(s0 == a_ref[r, h:h + 1, :], l_ref[r, h:h + 1, :], lrow)
                crank = jnp.where(s1 == a_ref[r, heads + h:heads + h + 1, :], float(r), crank)
            lr_ref[h] = lrow
            cr_ref[h] = crank.astype(cr_ref.dtype)

    @pl.when(redo)
    def _():
        for h in range(heads):
            rk0 = rk_ref[h]
            lrow = jnp.zeros((nk, tt), F32)
            for r in range(PEER_TOPK):
                lrow = jnp.where(rk0 == float(r), l_ref[r, h:h + 1, :], lrow)
            lr_ref[h] = lrow
            cr_ref[h] = rk_ref[heads + h].astype(cr_ref.dtype)


def _peer_select(q, keys):
    n, dq = q.shape
    heads, _, nk, dk = keys.shape
    assert nk == PEER_TOPK * SUBLANES and dk == LANES
    tt = _pick(n, (256, 128))
    out = jax.ShapeDtypeStruct((heads, nk, n), F32)
    out_j = jax.ShapeDtypeStruct((heads, nk, n), BF16)
    ospec = pl.BlockSpec((heads, nk, tt), lambda t: (0, 0, t))
    return pl.pallas_call(
        functools.partial(_peer_select_kernel, heads=heads),
        grid=(n // tt,),
        in_specs=[
            pl.BlockSpec((tt, dq), lambda t: (t, 0)),
            pl.BlockSpec((2 * heads, nk, dk), lambda t: (0, 0, 0)),
        ],
        out_specs=[ospec, ospec, ospec, ospec],
        out_shape=[out, out, out_j, out_j],
        scratch_shapes=[
            pltpu.VMEM((2 * heads, nk, tt), F32),
            pltpu.VMEM((2 * heads, nk, tt), F32),
            pltpu.VMEM((PEER_TOPK, 2 * heads, tt), F32),
            pltpu.VMEM((PEER_TOPK + 1, heads, tt), F32),
        ],
        name="peer_select",
        compiler_params=_params("parallel"),
    )(q, keys.reshape(2 * heads, nk, dk))


PEER_CHUNK = 1024


def _peer_dense_kernel(x_ref, g2_ref, h_ref, u_ref, vt_prev_ref, vt_cur_ref, vt_last_ref, ea_ref, lr_ref, eb_ref,
                       cr_ref, o_ref, acc_ref, act_ref, coef_lo_ref, coef_hi_ref, *, heads, group):
    tt = act_ref.shape[1]
    c = pl.program_id(1)
    one_trip = jnp.minimum(c, 0) + 1
    iph = PEER_CHUNK // 2 // LANES

    @pl.when(c == 0)
    def _():
        coef_hi_ref[...] = jnp.zeros_like(coef_hi_ref)
        acc_ref[...] = jnp.zeros_like(acc_ref)

    hrows = PEER_CHUNK // 2

    def first_matmul(half):
        rows = slice(half * hrows, (half + 1) * hrows)
        act_ref[rows, :] = lax.dot_general(u_ref[rows, :], h_ref[...], (((1,), (1,)), ((), ())),
                                           preferred_element_type=F32)

    first_matmul(0)

    def build(half, dst):
        tb = min(tt, 2 * LANES)
        for s in range(iph):
            il = half * iph + s
            rows = slice(il * LANES, (il + 1) * LANES)
            for t0 in range(0, tt, tb):
                cols = slice(t0, t0 + tb)
                wsum = jnp.zeros((LANES, tb), BF16)
                for h in range(heads):
                    lrow = lr_ref[h, il:il + 1, cols].astype(BF16)
                    ea = ea_ref[h, il:il + 1, cols].astype(BF16)
                    wsum = wsum + jnp.where(cr_ref[h, :, cols] < lrow, eb_ref[h, :, cols] * ea,
                                            jnp.zeros((), BF16))
                g = jax.nn.gelu(act_ref[rows, cols], approximate=True)
                dst[s * LANES:(s + 1) * LANES, cols] = (wsum.astype(F32) * g).astype(BF16)

    def region(half, vt_ref, src, dst):
        def body(_, carry):
            build(half, dst)
            acc_ref[...] += jnp.dot(vt_ref[0], src[...], preferred_element_type=F32)
            if half == 0:
                first_matmul(1)
            return carry
        lax.fori_loop(0, one_trip, body, 0)

    region(0, vt_prev_ref, coef_hi_ref, coef_lo_ref)
    region(1, vt_cur_ref, coef_lo_ref, coef_hi_ref)

    @pl.when(c == pl.num_programs(1) - 1)
    def _():
        acc_ref[...] += jnp.dot(vt_last_ref[0], coef_hi_ref[...], preferred_element_type=F32)
        y = acc_ref[...].T
        for g in range(tt // group):
            rs = slice(g * group, (g + 1) * group)
            o_ref[rs, :] = x_ref[rs, :] + g2_ref[g] * y[rs]


def _peer_dense(x, g2, h_bf16, u_bf16, v_bf16, ea, lr, eb, cr, *, group):
    n, d = h_bf16.shape
    e = u_bf16.shape[0]
    heads, nk, _ = ea.shape
    assert e % PEER_CHUNK == 0
    tt = _pick(n, (512, 256, 128))
    half = PEER_CHUNK // 2
    nhalves = e // half
    vt = v_bf16.reshape(nhalves, half, d).transpose(0, 2, 1)
    vt_half = lambda index: pl.BlockSpec((1, d, half), lambda t, c: (index(c), 0, 0))
    sel_i = pl.BlockSpec((heads, PEER_CHUNK // LANES, tt), lambda t, c: (0, c, t))
    sel_j = pl.BlockSpec((heads, nk, tt), lambda t, c: (0, 0, t))
    return pl.pallas_call(
        functools.partial(_peer_dense_kernel, heads=heads, group=group),
        grid=(n // tt, e // PEER_CHUNK),
        in_specs=[
            pl.BlockSpec((tt, d), lambda t, c: (t, 0)),
            pl.BlockSpec((tt // group, 1, d), lambda t, c: (t, 0, 0)),
            pl.BlockSpec((tt, d), lambda t, c: (t, 0)),
            pl.BlockSpec((PEER_CHUNK, d), lambda t, c: (c, 0)),
            vt_half(lambda c: jnp.maximum(2 * c - 1, 0)),
            vt_half(lambda c: 2 * c),
            vt_half(lambda c: nhalves - 1),
            sel_i, sel_i, sel_j, sel_j,
        ],
        out_specs=pl.BlockSpec((tt, d), lambda t, c: (t, 0)),
        out_shape=jax.ShapeDtypeStruct((n, d), F32),
        scratch_shapes=[pltpu.VMEM((d, tt), F32), pltpu.VMEM((PEER_CHUNK, tt), F32),
                        pltpu.VMEM((half, tt), BF16), pltpu.VMEM((half, tt), BF16)],
        name="peer_dense",
        compiler_params=_params("parallel", "arbitrary"),
    )(x, g2, h_bf16, u_bf16, vt, vt, vt, ea, lr, eb, cr)


def _final_norm_kernel(x_ref, g_ref, o_ref):
    x = x_ref[...]
    o_ref[...] = x * lax.rsqrt(jnp.mean(x * x, axis=-1, keepdims=True) + EPS) * g_ref[...]


def _final_norm(x, gain, row0, rows):
    d = x.shape[1]
    tr = _pick(math.gcd(row0, rows), (512, 256, 128, 64))
    first = row0 // tr
    return pl.pallas_call(
        _final_norm_kernel,
        grid=(rows // tr,),
        in_specs=[pl.BlockSpec((tr, d), lambda i: (first + i, 0)), pl.BlockSpec((1, d), lambda i: (0, 0))],
        out_specs=pl.BlockSpec((tr, d), lambda i: (i, 0)),
        out_shape=jax.ShapeDtypeStruct((rows, d), F32),
        name="final_norm",
        compiler_params=_params("parallel"),
    )(x, gain.reshape(1, d))


def kernel(x_prompt, x_sample, cache_k, cache_v, state_conv, c_prompt, c_sample, w_ada, b_ada, norm_mix,
           w_in, w_conv, norm_att_out, norm_conv_out, w_out, norm_ffn, w_peer_q, peer_keys, peer_u, peer_v,
           norm_final):
    bp, tp, d = x_prompt.shape
    bs, ts, _ = x_sample.shape
    depth = w_in.shape[0]
    past = cache_k.shape[2]
    heads = cache_k.shape[3]
    w = heads * LANES
    assert cache_k.shape[4] == LANES and state_conv.shape[-1] == w and w_in.shape[2] == 6 * w
    assert tp % ts == 0 and ts % SUBLANES == 0
    group = ts
    n_p, n_s = bp * tp, bs * ts
    n = n_p + n_s
    scale = float(LANES) ** -0.5

    x = jnp.concatenate([x_prompt.reshape(n_p, d), x_sample.reshape(n_s, d)], axis=0)

    c_all = jnp.concatenate([c_prompt, c_sample], axis=0)
    r = c_all.shape[0]
    r_pad = -(-r // SUBLANES) * SUBLANES
    mod = _ada(jnp.pad(c_all, ((0, r_pad - r), (0, 0))), w_ada, b_ada)
    group_row = jnp.concatenate([jnp.repeat(jnp.arange(bp), tp // group), bp + jnp.arange(bs)])
    mod = jnp.take(mod, group_row, axis=1).reshape(depth, n // group, 6, 1, d)

    state =jnp.concatenate([jnp.zeros((depth, n_p // group, CONV_K - 1, w), F32), state_conv], axis=1)
    seq_end_group = jnp.concatenate([(jnp.arange(bp) + 1) * (tp // group) - 1, n_p // group + jnp.arange(bs)])

    kp = jnp.zeros((depth, n_p, heads, LANES), F32)
    vp = jnp.zeros((depth, n_p, heads, LANES), F32)
    ks = jnp.zeros((depth, n_s, heads, LANES), F32)
    vs = jnp.zeros((depth, n_s, heads, LANES), F32)
    cp, cs = [], []
    for l in range(depth):
        sh1, sc1, g1, sh2, sc2, g2 = (mod[l, :, k] for k in range(6))
        proj = _norm_mod_matmul(x, norm_mix[l], sc1, sh1, w_in[l].astype(BF16), group=group, emit_h=False)
        att_p, kp, vp = _attn_prompt(proj, kp, vp, l, batch=bp, seq=tp, heads=heads, scale=scale)
        att_s, ks, vs = _attn_sample_cached(proj, cache_k, cache_v, l, ks, vs, n_prompt=n_p, seq=ts, scale=scale)
        x, tails = _mix_out(att_p, att_s, proj, state[l], w_conv[l], norm_att_out[l], norm_conv_out[l],
                            w_out[l].astype(BF16), x, g1, group=group, n_prompt=n_p, seq=tp)
        q, h2 = _norm_mod_matmul(x, norm_ffn[l], sc2, sh2, w_peer_q[l].astype(BF16), group=group, emit_h=True)
        ea, lr, eb, cr = _peer_select(q, peer_keys[l])
        x = _peer_dense(x, g2, h2, peer_u[l].astype(BF16), peer_v[l].astype(BF16), ea, lr, eb, cr, group=group)

        conv_new = jnp.take(tails, seq_end_group, axis=0)[:, SUBLANES - (CONV_K - 1):, :]
        cp.append(conv_new[:bp])
        cs.append(conv_new[bp:])

    y_p = _final_norm(x, norm_final, 0, n_p)
    y_s = _final_norm(x, norm_final, n_p, n_s)
    return (y_p.reshape(bp, tp, d), y_s.reshape(bs, ts, d),
            kp.reshape(depth, bp, tp, heads, LANES), vp.reshape(depth, bp, tp, heads, LANES), jnp.stack(cp),
            ks.reshape(depth, bs, ts, heads, LANES), vs.reshape(depth, bs, ts, heads, LANES), jnp.stack(cs))
```

```python
import functools
import math

import jax
import jax.numpy as jnp
from jax import lax
from jax.experimental import pallas as pl
from jax.experimental.pallas import tpu as pltpu

F32 = jnp.float32
BF16 = jnp.bfloat16
EPS = 1e-6
LANES = 128
SUBLANES = 8
CONV_K = 3
PEER_TOPK = 16
VMEM_LIMIT_BYTES = 56 * 1024 * 1024
EXP_UNDERFLOW = -104.0


def _params(*sem):
    return pltpu.CompilerParams(dimension_semantics=sem, vmem_limit_bytes=VMEM_LIMIT_BYTES)


def _pick(n, cands):
    for c in cands:
        if n % c == 0:
            return c
    raise ValueError(f"no tile in {cands} divides {n}")


def _ada_kernel(c_ref, w_ref, b_ref, o_ref):
    c = c_ref[...]
    s = (c * jax.nn.sigmoid(c)).astype(BF16)
    o_ref[...] = jnp.dot(s, w_ref[...].astype(BF16), preferred_element_type=F32) + b_ref[...]


def _ada(c_all, w_ada, b_ada):
    depth, d, n6 = w_ada.shape
    r = c_all.shape[0]
    tn = _pick(n6, (1024, 512, 256, 128))
    return pl.pallas_call(
        _ada_kernel,
        grid=(depth, n6 // tn),
        in_specs=[
            pl.BlockSpec((r, d), lambda l, j: (0, 0)),
            pl.BlockSpec((None, d, tn), lambda l, j: (l, 0, j)),
            pl.BlockSpec((None, 1, tn), lambda l, j: (l, 0, j)),
        ],
        out_specs=pl.BlockSpec((None, r, tn), lambda l, j: (l, 0, j)),
        out_shape=jax.ShapeDtypeStruct((depth, r, n6), F32),
        name="ada",
        compiler_params=_params("parallel", "parallel"),
    )(c_all, w_ada, b_ada.reshape(depth, 1, n6))


def _nmm_kernel(x_ref, g_ref, sc_ref, sh_ref, w_ref, o_ref, *rest, group, emit_h):
    hs_ref = rest[-1]
    tr = x_ref.shape[0]

    @pl.when(pl.program_id(1) == 0)
    def _():
        x = x_ref[...]
        y = x * lax.rsqrt(jnp.mean(x * x, axis=-1, keepdims=True) + EPS) * g_ref[...]
        for g in range(tr // group):
            rows = slice(g * group, (g + 1) * group)
            hs_ref[rows, :] = (y[rows] * (1.0 + sc_ref[g]) + sh_ref[g]).astype(BF16)
        if emit_h:
            rest[0][...] = hs_ref[...]

    o_ref[...] = jnp.dot(hs_ref[...], w_ref[...], preferred_element_type=F32)


def _norm_mod_matmul(x, gain, sc, sh, w_bf16, *, group, emit_h):
    n, d = x.shape
    nout = w_bf16.shape[1]
    tr = _pick(n, (512, 256, 128, 64))
    tn = _pick(nout, (2048, 1024, 512, 256, 128))
    gpt = tr // group
    out_shape = [jax.ShapeDtypeStruct((n, nout), F32)]
    out_specs = [pl.BlockSpec((tr, tn), lambda i, j: (i, j))]
    if emit_h:
        out_shape.append(jax.ShapeDtypeStruct((n, d), BF16))
        out_specs.append(pl.BlockSpec((tr, d), lambda i, j: (i, 0)))
    res = pl.pallas_call(
        functools.partial(_nmm_kernel, group=group, emit_h=emit_h),
        grid=(n // tr, nout // tn),
        in_specs=[
            pl.BlockSpec((tr, d), lambda i, j: (i, 0)),
            pl.BlockSpec((1, d), lambda i, j: (0, 0)),
            pl.BlockSpec((gpt, 1, d), lambda i, j: (i, 0, 0)),
            pl.BlockSpec((gpt, 1, d), lambda i, j: (i, 0, 0)),
            pl.BlockSpec((d, tn), lambda i, j: (0, j)),
        ],
        out_specs=out_specs,
        out_shape=out_shape,
        scratch_shapes=[pltpu.VMEM((tr, d), BF16)],
        name="norm_mod_matmul",
        compiler_params=_params("parallel", "arbitrary"),
    )(x, gain.reshape(1, d), sc, sh, w_bf16)
    return res if emit_h else res[0]


def _strict_upper(n):
    return (lax.broadcasted_iota(jnp.int32, (n, n), 0) > lax.broadcasted_iota(jnp.int32, (n, n), 1)).astype(BF16)


def _sb_blocks(qs, ks, vs, tri, carries, accs, *, masked, scale):
    n = len(qs)
    tq = qs[0].shape[0]
    zs = [lax.dot_general(qs[h], ks[h], (((1,), (1,)), ((), ())), preferred_element_type=F32) * scale
          for h in range(n)]
    ls = [jnp.minimum(z, 0.0) - jnp.log1p(jnp.exp(-jnp.abs(z))) for z in zs]
    lr = [l - z for l, z in zip(ls, zs)]
    if masked:
        shape = zs[0].shape
        vis = lax.broadcasted_iota(jnp.int32, shape, 1) < lax.broadcasted_iota(jnp.int32, shape, 0)
        lr = [jnp.where(vis, x, 0.0) for x in lr]
    lr_all = jnp.concatenate(lr, axis=0)
    hi = lr_all.astype(BF16)
    r1 = lr_all - hi.astype(F32)
    mid = r1.astype(BF16)
    lo = (r1 - mid.astype(F32)).astype(BF16)
    suf = (jnp.dot(hi, tri, preferred_element_type=F32) + jnp.dot(mid, tri, preferred_element_type=F32)
           + jnp.dot(lo, tri, preferred_element_type=F32))
    ws = []
    for h in range(n):
        w = jnp.exp(ls[h] + suf[h * tq:(h + 1) * tq] + carries[h])
        if masked:
            w = jnp.where(vis, w, 0.0)
        ws.append(w.astype(BF16))
    accs = [accs[h] + jnp.dot(ws[h], vs[h], preferred_element_type=F32) for h in range(n)]
    carries = [carries[h] + jnp.sum(lr[h], axis=1, keepdims=True) for h in range(n)]
    return carries, accs


def _attn_prompt_kernel(q_ref, k_ref, v_ref, *rest, heads, scale):
    o_ref, ko_ref, vo_ref = rest[-3:]
    tq = q_ref.shape[0]
    qi = pl.program_id(1)
    tri = _strict_upper(tq)
    cols = [slice(h * LANES, (h + 1) * LANES) for h in range(heads)]
    qs = [q_ref[:, cols[h]].astype(BF16) for h in range(heads)]
    own = pl.ds(pl.multiple_of(qi * tq, tq), tq)
    for h in range(heads):
        ko_ref[:, h, :] = k_ref[own, cols[h]]
        vo_ref[:, h, :] = v_ref[own, cols[h]]

    def all_heads(kb, carries, accs, masked):
        rows = pl.ds(pl.multiple_of(kb * tq, tq), tq)
        return _sb_blocks(qs, [k_ref[rows, cols[h]].astype(BF16) for h in range(heads)],
                          [v_ref[rows, cols[h]].astype(BF16) for h in range(heads)], tri, carries, accs,
                          masked=masked, scale=scale)

    carries, accs = all_heads(qi, [jnp.zeros((tq, 1), F32)] * heads, [jnp.zeros((tq, LANES), F32)] * heads, True)

    def cond(st):
        kb, carries, _ = st
        live = functools.reduce(jnp.maximum, carries)
        return jnp.logical_and(kb >= 0, jnp.max(live) > EXP_UNDERFLOW)

    def body(st):
        kb, carries, accs = st
        carries, accs = all_heads(kb, carries, accs, False)
        return kb - 1, carries, accs

    _, _, accs = lax.while_loop(cond, body, (qi - 1, carries, accs))
    for h in range(heads):
        o_ref[:, cols[h]] = accs[h]


def _attn_prompt(proj, k_all, v_all, layer, *, batch, seq, heads, scale):
    w = heads * LANES
    tq = _pick(seq, (128, 64))
    nq = seq // tq
    kv_spec = pl.BlockSpec((None, tq, heads, LANES), lambda b, i: (layer, b * nq + i, 0, 0))
    return pl.pallas_call(
        functools.partial(_attn_prompt_kernel, heads=heads, scale=scale),
        grid=(batch, nq),
        in_specs=[
            pl.BlockSpec((tq, w), lambda b, i: (b * nq + i, 0)),
            pl.BlockSpec((seq, w), lambda b, i: (b, 1)),
            pl.BlockSpec((seq, w), lambda b, i: (b, 2)),
            pl.BlockSpec(memory_space=pl.ANY),
            pl.BlockSpec(memory_space=pl.ANY),
        ],
        out_specs=[pl.BlockSpec((tq, w), lambda b, i: (b * nq + i, 0)), kv_spec, kv_spec],
        out_shape=[jax.ShapeDtypeStruct((batch * seq, w), F32),
                   jax.ShapeDtypeStruct(k_all.shape, F32), jax.ShapeDtypeStruct(v_all.shape, F32)],
        input_output_aliases={3: 1, 4: 2},
        name="attn_prompt",
        compiler_params=_params("parallel", "parallel"),
    )(proj, proj, proj, k_all, v_all)


def _attn_sample_kernel(q_ref, kn_ref, vn_ref, kc_ref, vc_ref, *rest, heads, tk, scale, emit_kv):
    if emit_kv:
        _, _, o_ref, cmax_ref, ko_ref, vo_ref, carry_ref, acc_ref = rest
    else:
        o_ref, cmax_ref, carry_ref, acc_ref = rest
    tq = q_ref.shape[0]
    tkp = kc_ref.shape[0]
    j = pl.program_id(1)

    def hs(h):
        return slice(h * LANES, (h + 1) * LANES)

    def qs():
        return [q_ref[:, hs(h)].astype(BF16) for h in range(heads)]

    def put(carries, accs):
        for h in range(heads):
            carry_ref[h] = jnp.broadcast_to(carries[h], (tq, LANES))
            acc_ref[:, hs(h)] = accs[h]

    @pl.when(j == 0)
    def _():
        put(*_sb_blocks(qs(), [kn_ref[:, hs(h)].astype(BF16) for h in range(heads)],
                        [vn_ref[:, hs(h)].astype(BF16) for h in range(heads)], _strict_upper(tq),
                        [jnp.zeros((tq, 1), F32)] * heads, [jnp.zeros((tq, LANES), F32)] * heads,
                        masked=True, scale=scale))
        if emit_kv:
            for h in range(heads):
                ko_ref[:, h, :] = kn_ref[:, hs(h)]
                vo_ref[:, h, :] = vn_ref[:, hs(h)]

    @pl.when(j > 0)
    def _():
        tri = _strict_upper(tk)
        for sb in reversed(range(tkp // tk)):
            live = functools.reduce(jnp.maximum, [carry_ref[h] for h in range(heads)])

            @pl.when(jnp.max(live) > EXP_UNDERFLOW)
            def _():
                rows = slice(sb * tk, (sb + 1) * tk)
                put(*_sb_blocks(qs(), [kc_ref[rows, h, :].astype(BF16) for h in range(heads)],
                                [vc_ref[rows, h, :].astype(BF16) for h in range(heads)], tri,
                                [carry_ref[h][:, 0:1] for h in range(heads)],
                                [acc_ref[:, hs(h)] for h in range(heads)], masked=False, scale=scale))

    @pl.when(j == pl.num_programs(1) - 1)
    def _():
        o_ref[...] = acc_ref[...]
        for h in range(heads):
            cmax_ref[h:h + 1, :] = jnp.max(carry_ref[h], axis=0, keepdims=True)


SAMPLE_CACHE_BLOCK = 512


def _attn_sample(proj, cache_k, cache_v, layer, kv_out, *, n_blocks, n_prompt, seq, scale):
    _, batch, past, heads, _ = cache_k.shape
    w = heads * LANES
    tkp = _pick(past, (SAMPLE_CACHE_BLOCK, 256, 128))
    tk = min(tkp, 256)
    last_block = past // tkp - 1
    row0 = n_prompt // seq
    emit_kv = kv_out is not None
    cache_spec = pl.BlockSpec((None, None, tkp, heads, LANES),
                              lambda b, j: (layer, b, last_block + 1 - jnp.maximum(j, 1), 0, 0))
    in_specs = [
        pl.BlockSpec((seq, w), lambda b, j: (row0 + b, 0)),
        pl.BlockSpec((seq, w), lambda b, j: (row0 + b, 1)),
        pl.BlockSpec((seq, w), lambda b, j: (row0 + b, 2)),
        cache_spec,
        cache_spec,
    ]
    out_specs = [
        pl.BlockSpec((seq, w), lambda b, j: (b, 0)),
        pl.BlockSpec((None, heads, LANES), lambda b, j: (b, 0, 0)),
    ]
    out_shape = [
        jax.ShapeDtypeStruct((batch * seq, w), F32),
        jax.ShapeDtypeStruct((batch, heads, LANES), F32),
    ]
    args = [proj, proj, proj, cache_k, cache_v]
    aliases = {}
    if emit_kv:
        kv_spec = pl.BlockSpec((None, seq, heads, LANES), lambda b, j: (layer, b, 0, 0))
        in_specs += [pl.BlockSpec(memory_space=pl.ANY)] * 2
        out_specs += [kv_spec, kv_spec]
        out_shape += [jax.ShapeDtypeStruct(a.shape, F32) for a in kv_out]
        args += list(kv_out)
        aliases = {5: 2, 6: 3}
    return pl.pallas_call(
        functools.partial(_attn_sample_kernel, heads=heads, tk=tk, scale=scale, emit_kv=emit_kv),
        grid=(batch, 1 + n_blocks),
        in_specs=in_specs,
        out_specs=out_specs,
        out_shape=out_shape,
        input_output_aliases=aliases,
        scratch_shapes=[pltpu.VMEM((heads, seq, LANES), F32), pltpu.VMEM((seq, w), F32)],
        name="attn_sample",
        compiler_params=_params("parallel", "arbitrary"),
    )(*args)


def _attn_sample_cached(proj, cache_k, cache_v, layer, k_all, v_all, *, n_prompt, seq, scale):
    past = cache_k.shape[2]
    kw = dict(n_prompt=n_prompt, seq=seq, scale=scale)
    n_all = past // _pick(past, (SAMPLE_CACHE_BLOCK, 256, 128))
    att, cmax, k_all, v_all = _attn_sample(proj, cache_k, cache_v, layer, (k_all, v_all), n_blocks=1, **kw)
    if n_all > 1:
        att = lax.cond(
            jnp.max(cmax) > EXP_UNDERFLOW,
            lambda ck, cv: _attn_sample(proj, ck, cv, layer, None, n_blocks=n_all, **kw)[0],
            lambda ck, cv: att,
            cache_k, cache_v)
    return att, k_all, v_all


def _mix_kernel(att_p_ref, att_s_ref, bg_ref, cg_ref, xc_ref, cgh_ref, xch_ref, st_ref, wc_ref, na_ref, nc_ref,
                wo_ref, x_ref, g1_ref, o_ref, tail_ref, um1_ref, um2_ref, mix_ref,
                *, group, prompt_groups, groups_per_seq):
    tr, w = bg_ref.shape
    gpt = tr // group
    i = pl.program_id(0)

    u = cg_ref[...] * xc_ref[...]
    uh = cgh_ref[...] * xch_ref[...]
    rows = lax.broadcasted_iota(jnp.int32, (tr, w), 0)
    h7 = uh[SUBLANES - 1:SUBLANES, :]
    h6 = uh[SUBLANES - 2:SUBLANES - 1, :]
    um1_ref[...] = jnp.where(rows == 0, h7, pltpu.roll(u, 1, axis=0))
    um2_ref[...] = jnp.where(rows == 0, h6, jnp.where(rows == 1, h7, pltpu.roll(u, 2, axis=0)))
    r8 = lax.broadcasted_iota(jnp.int32, (SUBLANES, w), 0)
    for g in range(gpt):
        gg = i * gpt + g
        starts_sequence = jnp.logical_or(gg >= prompt_groups, gg % groups_per_seq == 0)
        head = slice(g * group, g * group + SUBLANES)

        @pl.when(starts_sequence)
        def _():
            st0 = st_ref[g, 0:1, :]
            st1 = st_ref[g, 1:2, :]
            um1_ref[head, :] = jnp.where(r8 == 0, st1, um1_ref[head, :])
            um2_ref[head, :] = jnp.where(r8 == 0, st0, jnp.where(r8 == 1, st1, um2_ref[head, :]))

        tail_ref[g] = u[(g + 1) * group - SUBLANES:(g + 1) * group, :]

    conv = wc_ref[0:1, :] * um2_ref[...] + wc_ref[1:2, :] * um1_ref[...] + wc_ref[2:3, :] * u
    cv = bg_ref[...] * conv
    att = jnp.where(i * gpt < prompt_groups, att_p_ref[...], att_s_ref[...])
    for k in range(w // LANES):
        cs = slice(k * LANES, (k + 1) * LANES)
        a = att[:, cs]
        mix_ref[:, cs] = (a * lax.rsqrt(jnp.mean(a * a, axis=-1, keepdims=True) + EPS)
                          * na_ref[:, cs]).astype(BF16)
        c = cv[:, cs]
        mix_ref[:, w + k * LANES:w + (k + 1) * LANES] = (
            c * lax.rsqrt(jnp.mean(c * c, axis=-1, keepdims=True) + EPS) * nc_ref[:, cs]).astype(BF16)
    y = jnp.dot(mix_ref[...], wo_ref[...], preferred_element_type=F32)
    for g in range(gpt):
        rs = slice(g * group, (g + 1) * group)
        o_ref[rs, :] = x_ref[rs, :] + g1_ref[g] * y[rs]


def _mix_out(att_p, att_s, proj, state, w_conv, norm_att, norm_conv, w_out_bf16, x, g1, *, group, n_prompt, seq):
    n, d = x.shape
    w = att_p.shape[1]
    tr = _pick(math.gcd(n_prompt, n - n_prompt), (256, 128, 64))
    gpt = tr // group
    hb = tr // SUBLANES
    ptiles = n_prompt // tr
    kern = functools.partial(_mix_kernel, group=group, prompt_groups=n_prompt // group,
                             groups_per_seq=seq // group)
    halo = lambda col: pl.BlockSpec((SUBLANES, w), lambda i: (jnp.maximum(i * hb - 1, 0), col))
    return pl.pallas_call(
        kern,
        grid=(n // tr,),
        in_specs=[
            pl.BlockSpec((tr, w), lambda i: (jnp.minimum(i, ptiles - 1), 0)),
            pl.BlockSpec((tr, w), lambda i: (jnp.maximum(i - ptiles, 0), 0)),
            pl.BlockSpec((tr, w), lambda i: (i, 3)),
            pl.BlockSpec((tr, w), lambda i: (i, 4)),
            pl.BlockSpec((tr, w), lambda i: (i, 5)),
            halo(4),
            halo(5),
            pl.BlockSpec((gpt, CONV_K - 1, w), lambda i: (i, 0, 0)),
            pl.BlockSpec((CONV_K, w), lambda i: (0, 0)),
            pl.BlockSpec((1, w), lambda i: (0, 0)),
            pl.BlockSpec((1, w), lambda i: (0, 0)),
            pl.BlockSpec((2 * w, d), lambda i: (0, 0)),
            pl.BlockSpec((tr, d), lambda i: (i, 0)),
            pl.BlockSpec((gpt, 1, d), lambda i: (i, 0, 0)),
        ],
        out_specs=[
            pl.BlockSpec((tr, d), lambda i: (i, 0)),
            pl.BlockSpec((gpt, SUBLANES, w), lambda i: (i, 0, 0)),
        ],
        out_shape=[
            jax.ShapeDtypeStruct((n, d), F32),
            jax.ShapeDtypeStruct((n // group, SUBLANES, w), F32),
        ],
        scratch_shapes=[pltpu.VMEM((tr, w), F32), pltpu.VMEM((tr, w), F32), pltpu.VMEM((tr, 2 * w), BF16)],
        name="mix_out",
        compiler_params=_params("parallel"),
    )(att_p, att_s, proj, proj, proj, proj, proj, state, w_conv, norm_att.reshape(1, w), norm_conv.reshape(1, w),
      w_out_bf16, x, g1)


def _staircase():
    return [(r, c) for r in range(PEER_TOPK) for c in range(PEER_TOPK) if (r + 1) * (c + 1) <= PEER_TOPK]


def _sort16_network():
    pairs, n, p = [], 16, 1
    while p < n:
        k = p
        while k >= 1:
            for j in range(k % p, n - k, 2 * k):
                for i in range(min(k, n - j - k)):
                    if (i + j) // (2 * p) == (i + j + k) // (2 * p):
                        pairs.append((i + j, i + j + k))
            k //= 2
        p *= 2
    return pairs


def _top16_values(slabs):
    x = list(slabs)
    for i, j in _sort16_network():
        x[i], x[j] = jnp.maximum(x[i], x[j]), jnp.minimum(x[i], x[j])
    for shift in (4, 2, 1):
        z = [jnp.maximum(x[r], pltpu.roll(x[PEER_TOPK - 1 - r], shift, axis=0)) for r in range(PEER_TOPK)]
        for dist in (8, 4, 2, 1):
            for i in range(PEER_TOPK):
                if i & dist == 0:
                    z[i], z[i + dist] = jnp.maximum(z[i], z[i + dist]), jnp.minimum(z[i], z[i + dist])
        x = z
    return x


def _top16_ranked(s, a_ref, slot):
    nk, tt = s.shape
    kidx = lax.broadcasted_iota(jnp.int32, (nk, tt), 0).astype(F32)
    work = s
    rank = jnp.full((nk, tt), float(PEER_TOPK), F32)
    for r in range(PEER_TOPK):
        m = jnp.max(work, axis=0, keepdims=True)
        first = jnp.min(jnp.where(work == m, kidx, float(nk)), axis=0, keepdims=True)
        hit = kidx == first
        rank = jnp.where(hit, float(r), rank)
        work = jnp.where(hit, -jnp.inf, work)
        a_ref[r, pl.ds(slot, 1), :] = m
    return rank


def _kept_per_row(a_ref, l_ref, heads):
    cands = _staircase()
    a0 = [a_ref[r, 0:heads, :] for r in range(PEER_TOPK)]
    a1 = [a_ref[c, heads:2 * heads, :] for c in range(PEER_TOPK)]
    sums = [a0[r] + a1[c] for r, c in cands]
    ahead = [0.0] * len(cands)
    for xi, (rx, cx) in enumerate(cands):
        for yi in range(xi):
            ry, cy = cands[yi]
            if ry <= rx and cy <= cx:
                ahead[xi] = ahead[xi] + 1.0
            else:
                y_first = sums[yi] >= sums[xi]
                ahead[xi] = ahead[xi] + jnp.where(y_first, 1.0, 0.0)
                ahead[yi] = ahead[yi] + jnp.where(y_first, 0.0, 1.0)
    zsum = jnp.zeros_like(sums[0])
    kept = [jnp.zeros_like(sums[0]) for _ in range(PEER_TOPK)]
    for xi, (rx, cx) in enumerate(cands):
        keep = jnp.where(ahead[xi] < float(PEER_TOPK), 1.0, 0.0)
        kept[rx] = kept[rx] + keep
        zsum = zsum + keep * jnp.exp(sums[xi] - sums[0])
    for r in range(PEER_TOPK):
        l_ref[r] = kept[r]
    l_ref[PEER_TOPK] = 1.0 / zsum


def _peer_select_kernel(q_ref, key_ref, ea_ref, lr_ref, eb_ref, cr_ref, s_ref, rk_ref, a_ref, l_ref,
                        *, heads):
    nk, tt = s_ref.shape[1], s_ref.shape[2]
    nslab = nk // SUBLANES
    ambiguous = jnp.zeros((1, tt), F32)
    for hp in range(2 * heads):
        slot = (hp % 2) * heads + hp // 2
        s = lax.dot_general(key_ref[hp], q_ref[:, hp * LANES:(hp + 1) * LANES], (((1,), (1,)), ((), ())),
                            precision=lax.Precision.HIGHEST, preferred_element_type=F32)
        s_ref[slot] = s
        slabs = [s[v * SUBLANES:(v + 1) * SUBLANES, :] for v in range(nslab)]
        best = _top16_values(slabs)
        for r in range(PEER_TOPK):
            a_ref[r, slot:slot + 1, :] = best[r][0:1, :]
        at_least_last = jnp.zeros((SUBLANES, tt), F32)
        for v in range(nslab):
            at_least_last = at_least_last + jnp.where(slabs[v] >= best[PEER_TOPK - 1], 1.0, 0.0)
        bad = jnp.sum(at_least_last, axis=0, keepdims=True) != float(PEER_TOPK)
        for r in range(PEER_TOPK - 1):
            bad = jnp.logical_or(bad, best[r][0:1, :] == best[r + 1][0:1, :])
        ambiguous = jnp.maximum(ambiguous, jnp.where(bad, 1.0, 0.0))
    redo = jnp.max(ambiguous) > 0.0

    @pl.when(redo)
    def _():
        def one_slot(slot, carry):
            rk_ref[slot] = _top16_ranked(s_ref[slot], a_ref, slot)
            return carry
        lax.fori_loop(0, 2 * heads, one_slot, 0)

    _kept_per_row(a_ref, l_ref, heads)

    for h in range(heads):
        ea_ref[h] = jnp.exp(s_ref[h] - a_ref[0, h:h + 1, :]) * l_ref[PEER_TOPK, h:h + 1, :]
        eb_ref[h] = jnp.exp(s_ref[heads + h] - a_ref[0, heads + h:heads + h + 1, :]).astype(eb_ref.dtype)

    @pl.when(jnp.logical_not(redo))
    def _():
        for h in range(heads):
            s0, s1 = s_ref[h], s_ref[heads + h]
            lrow = jnp.zeros((nk, tt), F32)
            crank = jnp.full((nk, tt), float(PEER_TOPK), F32)
            for r in range(PEER_TOPK):
                lrow = jnp.where(s0 == a_ref[r, h:h + 1, :], l_ref[r, h:h + 1, :], lrow)
                crank = jnp.where(s1 == a_ref[r, heads + h:heads + h + 1, :], float(r), crank)
            lr_ref[h] = lrow
            cr_ref[h] = crank.astype(cr_ref.dtype)

    @pl.when(redo)
    def _():
        for h in range(heads):
            rk0 = rk_ref[h]
            lrow = jnp.zeros((nk, tt), F32)
            for r in range(PEER_TOPK):
                lrow = jnp.where(rk0 == float(r), l_ref[r, h:h + 1, :], lrow)
            lr_ref[h] = lrow
            cr_ref[h] = rk_ref[heads + h].astype(cr_ref.dtype)


def _peer_select(q, keys):
    n, dq = q.shape
    heads, _, nk, dk = keys.shape
    assert nk == PEER_TOPK * SUBLANES and dk == LANES
    tt = _pick(n, (256, 128))
    out = jax.ShapeDtypeStruct((heads, nk, n), F32)
    out_j = jax.ShapeDtypeStruct((heads, nk, n), BF16)
    ospec = pl.BlockSpec((heads, nk, tt), lambda t: (0, 0, t))
    return pl.pallas_call(
        functools.partial(_peer_select_kernel, heads=heads),
        grid=(n // tt,),
        in_specs=[
            pl.BlockSpec((tt, dq), lambda t: (t, 0)),
            pl.BlockSpec((2 * heads, nk, dk), lambda t: (0, 0, 0)),
        ],
        out_specs=[ospec, ospec, ospec, ospec],
        out_shape=[out, out, out_j, out_j],
        scratch_shapes=[
            pltpu.VMEM((2 * heads, nk, tt), F32),
            pltpu.VMEM((2 * heads, nk, tt), F32),
            pltpu.VMEM((PEER_TOPK, 2 * heads, tt), F32),
            pltpu.VMEM((PEER_TOPK + 1, heads, tt), F32),
        ],
        name="peer_select",
        compiler_params=_params("parallel"),
    )(q, keys.reshape(2 * heads, nk, dk))


PEER_CHUNK = 1024
GELU_C1 = -2.0 * math.sqrt(2.0 / math.pi)
GELU_C3 = GELU_C1 * 0.044715


def _peer_dense_kernel(x_ref, g2_ref, h_ref, u_ref, vt_prev_ref, vt_cur_ref, vt_last_ref, ea_ref, lr_ref, eb_ref,
                       cr_ref, o_ref, acc_ref, act_ref, coef_lo_ref, coef_hi_ref, *, heads, group):
    tt = act_ref.shape[1]
    c = pl.program_id(1)
    one_trip = jnp.minimum(c, 0) + 1
    iph = PEER_CHUNK // 2 // LANES

    @pl.when(c == 0)
    def _():
        coef_hi_ref[...] = jnp.zeros_like(coef_hi_ref)
        acc_ref[...] = jnp.zeros_like(acc_ref)

    hrows = PEER_CHUNK // 2

    def first_matmul(half):
        rows = slice(half * hrows, (half + 1) * hrows)
        act_ref[rows, :] = lax.dot_general(u_ref[rows, :], h_ref[...], (((1,), (1,)), ((), ())),
                                           preferred_element_type=F32)

    first_matmul(0)

    def build(half, dst):
        tb = min(tt, 2 * LANES)
        for s in range(iph):
            il = half * iph + s
            rows = slice(il * LANES, (il + 1) * LANES)
            for t0 in range(0, tt, tb):
                cols = slice(t0, t0 + tb)
                wsum = jnp.zeros((LANES, tb), BF16)
                for h in range(heads):
                    lrow = lr_ref[h, il:il + 1, cols].astype(BF16)
                    ea = ea_ref[h, il:il + 1, cols].astype(BF16)
                    wsum = wsum + jnp.where(cr_ref[h, :, cols] < lrow, eb_ref[h, :, cols] * ea,
                                            jnp.zeros((), BF16))
                a = act_ref[rows, cols]
                g = a / (1.0 + jnp.exp(a * (GELU_C3 * (a * a) + GELU_C1)))
                dst[s * LANES:(s + 1) * LANES, cols] = (wsum.astype(F32) * g).astype(BF16)

    def region(half, vt_ref, src, dst):
        def body(_, carry):
            build(half, dst)
            acc_ref[...] += jnp.dot(vt_ref[0], src[...], preferred_element_type=F32)
            if half == 0:
                first_matmul(1)
            return carry
        lax.fori_loop(0, one_trip, body, 0)

    region(0, vt_prev_ref, coef_hi_ref, coef_lo_ref)
    region(1, vt_cur_ref, coef_lo_ref, coef_hi_ref)

    @pl.when(c == pl.num_programs(1) - 1)
    def _():
        acc_ref[...] += jnp.dot(vt_last_ref[0], coef_hi_ref[...], preferred_element_type=F32)
        y = acc_ref[...].T
        for g in range(tt // group):
            rs = slice(g * group, (g + 1) * group)
            o_ref[rs, :] = x_ref[rs, :] + g2_ref[g] * y[rs]


def _peer_dense(x, g2, h_bf16, u_bf16, v_bf16, ea, lr, eb, cr, *, group):
    n, d = h_bf16.shape
    e = u_bf16.shape[0]
    heads, nk, _ = ea.shape
    assert e % PEER_CHUNK == 0
    tt = _pick(n, (512, 256, 128))
    half = PEER_CHUNK // 2
    nhalves = e // half
    vt = v_bf16.reshape(nhalves, half, d).transpose(0, 2, 1)
    vt_half = lambda index: pl.BlockSpec((1, d, half), lambda t, c: (index(c), 0, 0))
    sel_i = pl.BlockSpec((heads, PEER_CHUNK // LANES, tt), lambda t, c: (0, c, t))
    sel_j = pl.BlockSpec((heads, nk, tt), lambda t, c: (0, 0, t))
    return pl.pallas_call(
        functools.partial(_peer_dense_kernel, heads=heads, group=group),
        grid=(n // tt, e // PEER_CHUNK),
        in_specs=[
            pl.BlockSpec((tt, d), lambda t, c: (t, 0)),
            pl.BlockSpec((tt // group, 1, d), lambda t, c: (t, 0, 0)),
            pl.BlockSpec((tt, d), lambda t, c: (t, 0)),
            pl.BlockSpec((PEER_CHUNK, d), lambda t, c: (c, 0)),
            vt_half(lambda c: jnp.maximum(2 * c - 1, 0)),
            vt_half(lambda c: 2 * c),
            vt_half(lambda c: nhalves - 1),
            sel_i, sel_i, sel_j, sel_j,
        ],
        out_specs=pl.BlockSpec((tt, d), lambda t, c: (t, 0)),
        out_shape=jax.ShapeDtypeStruct((n, d), F32),
        scratch_shapes=[pltpu.VMEM((d, tt), F32), pltpu.VMEM((PEER_CHUNK, tt), F32),
                        pltpu.VMEM((half, tt), BF16), pltpu.VMEM((half, tt), BF16)],
        name="peer_dense",
        compiler_params=_params("parallel", "arbitrary"),
    )(x, g2, h_bf16, u_bf16, vt, vt, vt, ea, lr, eb, cr)


def _final_norm_kernel(x_ref, g_ref, o_ref):
    x = x_ref[...]
    o_ref[...] = x * lax.rsqrt(jnp.mean(x * x, axis=-1, keepdims=True) + EPS) * g_ref[...]


def _final_norm(x, gain, row0, rows):
    d = x.shape[1]
    tr = _pick(math.gcd(row0, rows), (512, 256, 128, 64))
    first = row0 // tr
    return pl.pallas_call(
        _final_norm_kernel,
        grid=(rows // tr,),
        in_specs=[pl.BlockSpec((tr, d), lambda i: (first + i, 0)), pl.BlockSpec((1, d), lambda i: (0, 0))],
        out_specs=pl.BlockSpec((tr, d), lambda i: (i, 0)),
        out_shape=jax.ShapeDtypeStruct((rows, d), F32),
        name="final_norm",
        compiler_params=_params("parallel"),
    )(x, gain.reshape(1, d))


def kernel(x_prompt, x_sample, cache_k, cache_v, state_conv, c_prompt, c_sample, w_ada, b_ada, norm_mix,
           w_in, w_conv, norm_att_out, norm_conv_out, w_out, norm_ffn, w_peer_q, peer_keys, peer_u, peer_v,
           norm_final):
    bp, tp, d = x_prompt.shape
    bs, ts, _ = x_sample.shape
    depth = w_in.shape[0]
    past = cache_k.shape[2]
    heads = cache_k.shape[3]
    w = heads * LANES
    assert cache_k.shape[4] == LANES and state_conv.shape[-1] == w and w_in.shape[2] == 6 * w
    assert tp % ts == 0 and ts % SUBLANES == 0
    group = ts
    n_p, n_s = bp * tp, bs * ts
    n = n_p + n_s
    scale = float(LANES) ** -0.5

    x = jnp.concatenate([x_prompt.reshape(n_p, d), x_sample.reshape(n_s, d)], axis=0)

    c_all = jnp.concatenate([c_prompt, c_sample], axis=0)
    r = c_all.shape[0]
    r_pad = -(-r // SUBLANES) * SUBLANES
    mod = _ada(jnp.pad(c_all, ((0, r_pad - r), (0, 0))), w_ada, b_ada)
    group_row = jnp.concatenate([jnp.repeat(jnp.arange(bp), tp // group), bp + jnp.arange(bs)])
    mod = jnp.take(mod, group_row, axis=1).reshape(depth, n // group, 6, 1, d)

    state =jnp.concatenate([jnp.zeros((depth, n_p // group, CONV_K - 1, w), F32), state_conv], axis=1)
    seq_end_group = jnp.concatenate([(jnp.arange(bp) + 1) * (tp // group) - 1, n_p // group + jnp.arange(bs)])

    kp = jnp.zeros((depth, n_p, heads, LANES), F32)
    vp = jnp.zeros((depth, n_p, heads, LANES), F32)
    ks = jnp.zeros((depth, n_s, heads, LANES), F32)
    vs = jnp.zeros((depth, n_s, heads, LANES), F32)
    cp, cs = [], []
    for l in range(depth):
        sh1, sc1, g1, sh2, sc2, g2 = (mod[l, :, k] for k in range(6))
        proj = _norm_mod_matmul(x, norm_mix[l], sc1, sh1, w_in[l].astype(BF16), group=group, emit_h=False)
        att_p, kp, vp = _attn_prompt(proj, kp, vp, l, batch=bp, seq=tp, heads=heads, scale=scale)
        att_s, ks, vs = _attn_sample_cached(proj, cache_k, cache_v, l, ks, vs, n_prompt=n_p, seq=ts, scale=scale)
        x, tails = _mix_out(att_p, att_s, proj, state[l], w_conv[l], norm_att_out[l], norm_conv_out[l],
                            w_out[l].astype(BF16), x, g1, group=group, n_prompt=n_p, seq=tp)
        q, h2 = _norm_mod_matmul(x, norm_ffn[l], sc2, sh2, w_peer_q[l].astype(BF16), group=group, emit_h=True)
        ea, lr, eb, cr = _peer_select(q, peer_keys[l])
        x = _peer_dense(x, g2, h2, peer_u[l].astype(BF16), peer_v[l].astype(BF16), ea, lr, eb, cr, group=group)

        conv_new = jnp.take(tails, seq_end_group, axis=0)[:, SUBLANES - (CONV_K - 1):, :]
        cp.append(conv_new[:bp])
        cs.append(conv_new[bp:])

    y_p = _final_norm(x, norm_final, 0, n_p)
    y_s = _final_norm(x, norm_final, n_p, n_s)
    return (y_p.reshape(bp, tp, d), y_s.reshape(bs, ts, d),
            kp.reshape(depth, bp, tp, heads, LANES), vp.reshape(depth, bp, tp, heads, LANES), jnp.stack(cp),
            ks.reshape(depth, bs, ts, heads, LANES), vs.reshape(depth, bs, ts, heads, LANES), jnp.stack(cs))
```
